```python
import math
import jax
import jax.numpy as jnp
from jax import lax
import numpy as np

D_MODEL = 1024
BATCH = 16
SEQ = 2048
DEPTH = 2
DEC_BATCH = 32
DEC_SEQ = 32
PAST_LEN = 1024

CHUNK = 64
REC_BLOCK = CHUNK // 4
HG_HEADS = 4
HG_DK = 128
HG_DV = 128
GDN_HEADS = 4
GDN_DK = 128
GDN_DV = 128
CONV_W = 4
D_FF = -(-(8 * D_MODEL) // (3 * 256)) * 256
N_MOD = 6
EPS = 1e-6
HG_QK = HG_HEADS * HG_DK
HG_V = HG_HEADS * HG_DV
GDN_QK = GDN_HEADS * GDN_DK
GDN_V = GDN_HEADS * GDN_DV
CONV_CH = 2 * GDN_QK + GDN_V
IN_SIZES = (HG_QK, HG_QK, HG_V, HG_V, GDN_QK, GDN_QK, GDN_V, GDN_HEADS, GDN_HEADS, GDN_V, D_MODEL, D_MODEL)
D_IN = 2 * HG_QK + 2 * HG_V + 2 * GDN_QK + 2 * GDN_V + 2 * GDN_HEADS + 2 * D_MODEL

kernel_name = "hybrid_hgrn2_gdn_streaming_step"


def rmsnorm(x, w):
    xf = x.astype(jnp.float32)
    y = xf * lax.rsqrt(jnp.mean(jnp.square(xf), axis=-1, keepdims=True) + EPS)
    return (y * w.astype(jnp.float32)).astype(x.dtype)


def l2norm(x):
    xf = x.astype(jnp.float32)
    return xf * lax.rsqrt(jnp.sum(jnp.square(xf), axis=-1, keepdims=True) + EPS)


def split_cols(a, sizes):
    out, start = [], 0
    for s in sizes:
        out.append(a[..., start:start + s])
        start += s
    return out


def to_blocks(a, n_blocks):
    B, T = a.shape[:2]
    pad = [(0, 0), (0, n_blocks * REC_BLOCK - T)] + [(0, 0)] * (a.ndim - 2)
    a = jnp.pad(a, pad).reshape((B, n_blocks, REC_BLOCK) + a.shape[2:])
    return jnp.moveaxis(a, (1, 3), (0, 2))


def from_blocks(o, T):
    n, B, H, L, d = o.shape
    return o.transpose(1, 0, 3, 2, 4).reshape(B, n * L, H, d)[:, :T]


def chunk_gla(q, k, v, log_f, s0):
    T = q.shape[1]
    n = -(-T // REC_BLOCK)
    incl = jnp.tril(jnp.ones((REC_BLOCK, REC_BLOCK), dtype=bool))
    blocks = tuple(to_blocks(a.astype(jnp.float32), n) for a in (q, k, v, log_f))

    def step(S, blk):
        qc, kc, vc, gc = blk
        G = jnp.cumsum(gc, axis=2)
        rel = jnp.exp(jnp.where(incl[:, :, None], G[:, :, :, None, :] - G[:, :, None, :, :], -jnp.inf))
        A = jnp.einsum("bhtd,bhsd,bhtsd->bhts", qc, kc, rel)
        o = jnp.einsum("bhtd,bhdv->bhtv", qc * jnp.exp(G), S) + jnp.einsum("bhts,bhsv->bhtv", A, vc)
        G_last = G[:, :, -1:, :]
        S = jnp.exp(G_last[:, :, 0, :, None]) * S + jnp.einsum("bhsd,bhsv->bhdv", kc * jnp.exp(G_last - G), vc)
        return S, o

    S, o = lax.scan(step, s0.astype(jnp.float32), blocks)
    return from_blocks(o, T), S


def chunk_gated_delta(q, k, v, log_a, beta, s0):
    T = q.shape[1]
    n = -(-T // REC_BLOCK)
    dv = v.shape[-1]
    incl = jnp.tril(jnp.ones((REC_BLOCK, REC_BLOCK), dtype=bool))
    strict = jnp.tril(jnp.ones((REC_BLOCK, REC_BLOCK), dtype=bool), -1)
    eye = jnp.eye(REC_BLOCK, dtype=jnp.float32)
    blocks = tuple(to_blocks(a.astype(jnp.float32), n) for a in (q, k, v, log_a, beta))

    def step(S, blk):
        qc, kc, vc, gc, bc = blk
        G = jnp.cumsum(gc, axis=-1)
        rel = jnp.exp(jnp.where(incl, G[..., :, None] - G[..., None, :], -jnp.inf))
        M = jnp.where(strict, bc[..., :, None] * rel * jnp.einsum("bhtd,bhsd->bhts", kc, kc), 0.0)
        rhs = jnp.concatenate([bc[..., None] * vc, (bc * jnp.exp(G))[..., None] * kc], axis=-1)
        sol = lax.linalg.triangular_solve(eye + M, rhs, left_side=True, lower=True, unit_diagonal=True)
        u = sol[..., :dv] - jnp.einsum("bhtd,bhdv->bhtv", sol[..., dv:], S)
        qk = jnp.einsum("bhtd,bhsd->bhts", qc, kc) * rel
        o = jnp.einsum("bhtd,bhdv->bhtv", qc * jnp.exp(G)[..., None], S) + jnp.einsum("bhts,bhsv->bhtv", qk, u)
        S = jnp.exp(G[..., -1])[..., None, None] * S + jnp.einsum(
            "bhsd,bhsv->bhdv", kc * jnp.exp(G[..., -1:] - G)[..., None], u)
        return S, o

    S, o = lax.scan(step, s0.astype(jnp.float32), blocks)
    return from_blocks(o, T), S


def causal_conv(u, buf, w):
    T = u.shape[1]
    ext = jnp.concatenate([buf.astype(u.dtype), u], axis=1)
    y = ext[:, 0:T] * w[0]
    for j in range(1, CONV_W):
        y = y + ext[:, j:j + T] * w[j]
    return jax.nn.silu(y), ext[:, -(CONV_W - 1):]


def mixer(h, lb, s_hg, s_gdn, s_conv, w_in, hg_norm, conv_w, a_log, dt_bias, gdn_norm, w_pa, w_pb, w_out):
    B, T, _ = h.shape
    proj = h @ w_in
    hq, hf, hi, hog, gq, gk, gv, ga, gb, gz, gate_a, gate_b = split_cols(proj, IN_SIZES)
    heads = lambda a, d: a.reshape(B, T, -1, d)

    hf = hf.astype(jnp.float32)
    lb = lb.astype(jnp.float32)
    log_f = jnp.logaddexp(jnp.log(lb), jnp.log1p(-lb) + jax.nn.log_sigmoid(hf))
    k_hg = (1.0 - lb) * jax.nn.sigmoid(-hf)
    o_hg, s_hg_new = chunk_gla(heads(hq.astype(jnp.float32), HG_DK) * HG_DK ** -0.5, heads(k_hg, HG_DK),
                               heads(hi, HG_DV), heads(log_f, HG_DK), s_hg)
    o_hg = rmsnorm(o_hg, hg_norm) * jax.nn.silu(heads(hog, HG_DV).astype(jnp.float32))

    conv_out, s_conv_new = causal_conv(jnp.concatenate([gq, gk, gv], axis=-1), s_conv, conv_w)
    cq, ck, cv = split_cols(conv_out, (GDN_QK, GDN_QK, GDN_V))
    q = l2norm(heads(cq, GDN_DK)) * GDN_DK ** -0.5
    k = l2norm(heads(ck, GDN_DK))
    v = heads(cv, GDN_DV)
    log_a = -jnp.exp(a_log.astype(jnp.float32)) * jax.nn.softplus(ga.astype(jnp.float32) + dt_bias.astype(jnp.float32))
    beta = jax.nn.sigmoid(gb.astype(jnp.float32))
    o_gdn, s_gdn_new = chunk_gated_delta(q, k, v, log_a, beta, s_gdn)
    o_gdn = rmsnorm(o_gdn, gdn_norm) * jax.nn.silu(heads(gz, GDN_DV).astype(jnp.float32))

    y_a = o_hg.reshape(B, T, HG_V).astype(h.dtype) @ w_pa
    y_b = o_gdn.reshape(B, T, GDN_V).astype(h.dtype) @ w_pb
    y = jax.nn.sigmoid(gate_a) * y_a + jax.nn.sigmoid(gate_b) * y_b
    return y @ w_out, s_hg_new, s_gdn_new, s_conv_new


def swiglu(h, w_up, w_down):
    gate, up = jnp.split(h @ w_up, 2, axis=-1)
    return (jax.nn.silu(gate) * up) @ w_down


def trunk(x, c, s_hg, s_gdn, s_conv, lbs, w_ada, b_ada, norm_mix, w_in, hg_norm, conv_w, gdn_a_log,
          gdn_dt_bias, gdn_norm, w_proj_a, w_proj_b, w_out, norm_ffn, w_up, w_down, final_norm):
    new_hg, new_gdn, new_conv = [], [], []
    for l in range(DEPTH):
        mod = jax.nn.silu(c) @ w_ada[l] + b_ada[l]
        sh1, sc1, g1, sh2, sc2, g2 = [m[:, None, :] for m in jnp.split(mod, N_MOD, axis=-1)]
        h = rmsnorm(x, norm_mix[l]) * (1.0 + sc1) + sh1
        y, hg, gd, cv = mixer(h, lbs[l], s_hg[l], s_gdn[l], s_conv[l], w_in[l], hg_norm[l], conv_w[l],
                              gdn_a_log[l], gdn_dt_bias[l], gdn_norm[l], w_proj_a[l], w_proj_b[l], w_out[l])
        x = x + g1 * y
        h = rmsnorm(x, norm_ffn[l]) * (1.0 + sc2) + sh2
        x = x + g2 * swiglu(h, w_up[l], w_down[l])
        new_hg.append(hg)
        new_gdn.append(gd)
        new_conv.append(cv)
    return rmsnorm(x, final_norm), jnp.stack(new_hg), jnp.stack(new_gdn), jnp.stack(new_conv)


def setup_inputs(seed: int = 0) -> dict:
    key = jax.random.key(seed)
    ks = jax.random.split(key, 24)
    nrm = lambda k, shape, scale: jax.random.normal(k, shape, jnp.float32) * scale
    dt = jnp.exp(jax.random.uniform(ks[15], (DEPTH, GDN_HEADS), jnp.float32, math.log(1e-3), math.log(1e-1)))
    return {
        "x_prompt": nrm(ks[0], (BATCH, SEQ, D_MODEL), 1.0),
        "x_sample": nrm(ks[1], (DEC_BATCH, DEC_SEQ, D_MODEL), 1.0),
        "state_hgrn": nrm(ks[2], (DEPTH, DEC_BATCH, HG_HEADS, HG_DK, HG_DV), 0.5),
        "state_gdn": nrm(ks[3], (DEPTH, DEC_BATCH, GDN_HEADS, GDN_DK, GDN_DV), 0.1),
        "state_conv": nrm(ks[4], (DEPTH, DEC_BATCH, CONV_W - 1, CONV_CH), 1.0),
        "c_prompt": nrm(ks[5], (BATCH, D_MODEL), 1.0),
        "c_sample": nrm(ks[6], (DEC_BATCH, D_MODEL), 1.0),
        "w_ada": nrm(ks[7], (DEPTH, D_MODEL, N_MOD * D_MODEL), 0.5 * D_MODEL ** -0.5),
        "b_ada": nrm(ks[8], (DEPTH, N_MOD * D_MODEL), 0.02),
        "norm_mix": 1.0 + nrm(ks[9], (DEPTH, D_MODEL), 0.02),
        "w_in": nrm(ks[10], (DEPTH, D_MODEL, D_IN), D_MODEL ** -0.5),
        "hg_lb": nrm(ks[11], (DEPTH, HG_QK), 1.0),
        "hg_norm": 1.0 + nrm(ks[12], (DEPTH, HG_DV), 0.02),
        "conv_w": nrm(ks[13], (DEPTH, CONV_W, CONV_CH), CONV_W ** -0.5),
        "gdn_a_log": jnp.log(jax.random.uniform(ks[14], (DEPTH, GDN_HEADS), jnp.float32, 1.0, 16.0)),
        "gdn_dt_bias": dt + jnp.log(-jnp.expm1(-dt)),
        "gdn_norm": 1.0 + nrm(ks[16], (DEPTH, GDN_DV), 0.02),
        "w_proj_a": nrm(ks[17], (DEPTH, HG_V, D_MODEL), HG_V ** -0.5),
        "w_proj_b": nrm(ks[18], (DEPTH, GDN_V, D_MODEL), GDN_V ** -0.5),
        "w_out": nrm(ks[19], (DEPTH, D_MODEL, D_MODEL), D_MODEL ** -0.5),
        "norm_ffn": 1.0 + nrm(ks[20], (DEPTH, D_MODEL), 0.02),
        "w_up": nrm(ks[21], (DEPTH, D_MODEL, 2 * D_FF), D_MODEL ** -0.5),
        "w_down": nrm(ks[22], (DEPTH, D_FF, D_MODEL), D_FF ** -0.5),
        "final_norm": 1.0 + nrm(ks[23], (D_MODEL,), 0.02),
    }


def reference(x_prompt, x_sample, state_hgrn, state_gdn, state_conv, c_prompt, c_sample, w_ada, b_ada,
              norm_mix, w_in, hg_lb, hg_norm, conv_w, gdn_a_log, gdn_dt_bias, gdn_norm, w_proj_a, w_proj_b,
              w_out, norm_ffn, w_up, w_down, final_norm):
    lbs = jnp.cumsum(jax.nn.softmax(hg_lb.astype(jnp.float32), axis=0), axis=0)
    lbs = lbs - lbs[0:1]
    weights = (w_ada, b_ada, norm_mix, w_in, hg_norm, conv_w, gdn_a_log, gdn_dt_bias, gdn_norm,
               w_proj_a, w_proj_b, w_out, norm_ffn, w_up, w_down, final_norm)

    B = x_prompt.shape[0]
    zero_hg = jnp.zeros((DEPTH, B, HG_HEADS, HG_DK, HG_DV), jnp.float32)
    zero_gdn = jnp.zeros((DEPTH, B, GDN_HEADS, GDN_DK, GDN_DV), jnp.float32)
    zero_conv = jnp.zeros((DEPTH, B, CONV_W - 1, CONV_CH), x_prompt.dtype)
    y_prompt, hg_p, gdn_p, conv_p = trunk(x_prompt, c_prompt, zero_hg, zero_gdn, zero_conv, lbs, *weights)

    y_sample, hg_s, gdn_s, conv_s = trunk(x_sample, c_sample, state_hgrn, state_gdn, state_conv, lbs, *weights)

    return (y_prompt, y_sample, hg_p, gdn_p, conv_p, hg_s, gdn_s, conv_s)
```

```python
import functools
import math

import jax
import jax.numpy as jnp
from jax import lax
from jax.experimental import pallas as pl
from jax.experimental.pallas import tpu as pltpu

D_MODEL = 1024
DEPTH = 2
HEADS = 4
DK = 128
DV = 128
QK = HEADS * DK
CONV_W = 4
CONV_CH = 3 * QK
D_FF = 2816
N_MOD = 6
EPS = 1e-6
PAD = 8

_C_HG = 0
_C_CONV = 4 * QK
_C_AB = _C_CONV + CONV_CH
_C_REST = _C_AB + 2 * HEADS
D_IN = _C_REST + QK + 2 * D_MODEL

VMEM_LIMIT = 56 * 1024 * 1024

_F32 = jnp.float32
_BF16 = jnp.bfloat16


def _dot(a, b):
    return jnp.dot(a.astype(_BF16), b.astype(_BF16), preferred_element_type=_F32)


def _dot_nt(a, b):
    return lax.dot_general(a.astype(_BF16), b.astype(_BF16), (((1,), (1,)), ((), ())),
                           preferred_element_type=_F32)


def _dot_tn(a, b):
    return lax.dot_general(a.astype(_BF16), b.astype(_BF16), (((0,), (0,)), ((), ())),
                           preferred_element_type=_F32)


def _dot_hl(a, b):
    a_hi = a.astype(_BF16)
    a_lo = (a - a_hi.astype(_F32)).astype(_BF16)
    b_hi = b.astype(_BF16)
    b_lo = (b - b_hi.astype(_F32)).astype(_BF16)
    lhs = jnp.concatenate([a_hi, a_hi, a_lo], axis=1)
    rhs = jnp.concatenate([b_hi, b_lo, b_hi], axis=0)
    return jnp.dot(lhs, rhs, preferred_element_type=_F32)


def _split3(x):
    hi = x.astype(_BF16)
    r1 = x - hi.astype(_F32)
    mid = r1.astype(_BF16)
    lo = (r1 - mid.astype(_F32)).astype(_BF16)
    return hi, mid, lo


def _cumsum_rows(tri, x):
    hi, mid, lo = _split3(x)
    d = lambda p: jnp.dot(tri, p, preferred_element_type=_F32)
    return d(hi) + d(mid) + d(lo)


def _sigmoid(x):
    return 1.0 / (1.0 + jnp.exp(-x))


def _silu(x):
    return x * _sigmoid(x)


def _softplus(x):
    return jnp.maximum(x, 0.0) + jnp.log1p(jnp.exp(-jnp.abs(x)))


def _rms_rows(x):
    return x * lax.rsqrt(jnp.mean(x * x, axis=-1, keepdims=True) + EPS)


def _bcast_rows(ref, row, bt, ts):
    v = ref[:, row:row + 1, :]
    return jnp.broadcast_to(v, (bt, ts, v.shape[-1])).reshape(bt * ts, v.shape[-1])


def _block_row_bcast(x, block, row):
    r, n = x.shape
    x3 = x.reshape(r // block, block, n)
    return jnp.broadcast_to(x3[:, row:row + 1, :], x3.shape).reshape(r, n)


def _level_reference(g, m, rows):
    if 2 * m >= 8:
        return _block_row_bcast(g, 2 * m, m - 1)
    r = g.shape[0]
    pos = rows & (2 * m - 1)
    out = g
    for p in range(2 * m):
        shift = p - (m - 1)
        if shift == 0:
            continue
        out = jnp.where(pos == p, pltpu.roll(g, shift % r, axis=0), out)
    return out


def _mod_kernel(c_ref, w_ref, b_ref, o_ref):
    a = _silu(c_ref[...])
    o_ref[...] = _dot(a, w_ref[...]) + b_ref[...]


def _modulation(c_all, w_ada, b_ada):
    rows = c_all.shape[0]
    tn = 1024
    n = N_MOD * D_MODEL
    return pl.pallas_call(
        _mod_kernel,
        grid=(DEPTH, n // tn),
        in_specs=[
            pl.BlockSpec((rows, D_MODEL), lambda l, j: (0, 0)),
            pl.BlockSpec((None, D_MODEL, tn), lambda l, j: (l, 0, j)),
            pl.BlockSpec((None, 1, tn), lambda l, j: (l, 0, j)),
        ],
        out_specs=pl.BlockSpec((None, rows, tn), lambda l, j: (l, 0, j)),
        out_shape=jax.ShapeDtypeStruct((DEPTH, rows, n), _F32),
        compiler_params=pltpu.CompilerParams(dimension_semantics=("arbitrary", "arbitrary")),
        name="adaln_modulation",
    )(c_all, w_ada, b_ada.reshape(DEPTH, 1, n))


def _mixer_kernel(*refs, layer, bt, ts, chunk, carry):
    if carry:
        (x_ref, mod_ref, lb_ref, nmix_ref, w1_ref, wab_ref, w2_ref, hgn_ref, cw_ref, alog_ref, dtb_ref,
         gdn_ref, wpa_ref, wpb_ref, wout_ref,
         xo_ref, shg_ref, sgd_ref, cvo_ref, ext_ref, ohg_ref, ogd_ref) = refs
    else:
        (x_ref, mod_ref, shg0_ref, sgd0_ref, cv0_ref, lb_ref, nmix_ref, w1_ref, wab_ref, w2_ref, hgn_ref,
         cw_ref, alog_ref, dtb_ref, gdn_ref, wpa_ref, wpb_ref, wout_ref,
         xo_ref, shg_ref, sgd_ref, cvo_ref, ext_ref, ohg_ref, ogd_ref) = refs

    r = bt * ts
    n_chunks = r // chunk
    chunks_per_seq = ts // chunk

    if carry:
        @pl.when(pl.program_id(1) == 0)
        def _():
            shg_ref[...] = jnp.zeros_like(shg_ref)
            sgd_ref[...] = jnp.zeros_like(sgd_ref)
            ext_ref[:, PAD - (CONV_W - 1):PAD, :] = jnp.zeros((bt, CONV_W - 1, CONV_CH), _F32)
    else:
        shg_ref[...] = shg0_ref[...]
        sgd_ref[...] = sgd0_ref[...]
        ext_ref[:, PAD - (CONV_W - 1):PAD, :] = cv0_ref[...]

    x = x_ref[...].reshape(r, D_MODEL)
    sh1 = _bcast_rows(mod_ref, 0, bt, ts)
    sc1 = _bcast_rows(mod_ref, 1, bt, ts)
    h = _rms_rows(x) * nmix_ref[...] * (1.0 + sc1) + sh1
    hb = h.astype(_BF16)

    rows = lax.broadcasted_iota(jnp.int32, (r, 1), 0)
    rr = lax.broadcasted_iota(jnp.int32, (r, r), 0)
    cc = lax.broadcasted_iota(jnp.int32, (r, r), 1)
    shift_c = int(math.log2(chunk))
    tri = jnp.where(((rr >> shift_c) == (cc >> shift_c)) & (cc <= rr), 1.0, 0.0).astype(_BF16)

    ci = lax.broadcasted_iota(jnp.int32, (chunk, chunk), 0)
    cj = lax.broadcasted_iota(jnp.int32, (chunk, chunk), 1)
    incl = cj <= ci
    strict = cj < ci
    eye_c = jnp.where(ci == cj, 1.0, 0.0)

    lbp = lb_ref[...]
    e = jnp.exp(lbp - jnp.max(lbp, axis=0, keepdims=True))
    sm = e / jnp.sum(e, axis=0, keepdims=True)
    cs0 = sm[0:1, :]
    csl = cs0
    for i in range(1, layer + 1):
        csl = csl + sm[i:i + 1, :]
    lb = csl - cs0

    hq = _dot(hb, w1_ref[:, 0:QK]) * DK ** -0.5
    hf = _dot(hb, w1_ref[:, QK:2 * QK])
    hi = _dot(hb, w1_ref[:, 2 * QK:3 * QK])
    log_sig = jnp.minimum(hf, 0.0) - jnp.log1p(jnp.exp(-jnp.abs(hf)))
    la = jnp.log(lb)
    lbv = jnp.log1p(-lb) + log_sig
    log_f = jnp.maximum(la, lbv) + jnp.log1p(jnp.exp(-jnp.abs(la - lbv)))
    k_hg = (1.0 - lb) * (1.0 / (1.0 + jnp.exp(hf)))

    g_hg = _cumsum_rows(tri, log_f)
    g_last = _block_row_bcast(g_hg, chunk, chunk - 1)
    qg = hq * jnp.exp(g_hg)
    kd = k_hg * jnp.exp(g_last - g_hg)
    dec_hg = jnp.exp(g_last)

    levels = []
    m = chunk // 2
    while m >= 1:
        ref_g = _level_reference(g_hg, m, rows)
        ev = jnp.exp(-jnp.abs(g_hg - ref_g))
        sh = int(math.log2(m))
        mask = (((ci >> (sh + 1)) == (cj >> (sh + 1))) & (((ci >> sh) & 1) == 1) & (((cj >> sh) & 1) == 0))
        levels.append(((hq * ev).astype(_BF16), (k_hg * ev).astype(_BF16), mask))
        m //= 2
    hq_b = hq.astype(_BF16)
    khg_b = k_hg.astype(_BF16)

    for c in range(n_chunks):
        sq = c // chunks_per_seq
        r0 = c * chunk
        for hd in range(HEADS):
            rs = slice(r0, r0 + chunk)
            ls = slice(hd * DK, (hd + 1) * DK)
            a = eye_c * _dot_nt(hq_b[rs, ls], khg_b[rs, ls])
            for (qe, ke, mask) in levels:
                a = a + jnp.where(mask, _dot_nt(qe[rs, ls], ke[rs, ls]), 0.0)
            s = shg_ref[sq, hd]
            v = hi[rs, ls]
            o = _dot(qg[rs, ls], s) + _dot(a, v)
            dcol = jnp.transpose(jnp.broadcast_to(dec_hg[r0:r0 + 1, ls], (DK, DK)))
            shg_ref[sq, hd] = dcol * s + _dot_tn(kd[rs, ls], v)
            ohg_ref[rs, ls] = o

    u_pre = _dot(hb, w1_ref[:, _C_CONV:_C_CONV + CONV_CH]).reshape(bt, ts, CONV_CH)
    ext_ref[:, PAD:PAD + ts, :] = u_pre
    cw = cw_ref[...]
    base = PAD - (CONV_W - 1)
    y = ext_ref[:, base:base + ts, :] * cw[0:1, :]
    for j in range(1, CONV_W):
        y = y + ext_ref[:, base + j:base + j + ts, :] * cw[j:j + 1, :]
    tail = ext_ref[:, PAD + ts - (CONV_W - 1):PAD + ts, :]
    cvo_ref[...] = tail
    ext_ref[:, PAD - (CONV_W - 1):PAD, :] = tail
    conv = _silu(y).reshape(r, CONV_CH)

    gab = _dot(hb, wab_ref[...])
    log_a = -jnp.exp(alog_ref[...]) * _softplus(gab + dtb_ref[...])
    beta = _sigmoid(gab)
    g_gd = _cumsum_rows(tri, log_a)
    g_gd_t = jnp.transpose(g_gd)

    def l2n(z):
        return z * lax.rsqrt(jnp.sum(z * z, axis=-1, keepdims=True) + EPS)

    for c in range(n_chunks):
        sq = c // chunks_per_seq
        r0 = c * chunk
        rs = slice(r0, r0 + chunk)
        for hd in range(HEADS):
            q = l2n(conv[rs, hd * DK:(hd + 1) * DK]) * DK ** -0.5
            k = l2n(conv[rs, QK + hd * DK:QK + (hd + 1) * DK])
            v = conv[rs, 2 * QK + hd * DV:2 * QK + (hd + 1) * DV]
            gc = g_gd[rs, hd:hd + 1]
            gr = g_gd_t[hd:hd + 1, rs]
            bc = beta[rs, HEADS + hd:HEADS + hd + 1]
            rel = jnp.exp(jnp.where(incl, gc - gr, -jnp.inf))
            kb = k.astype(_BF16)
            p = jnp.where(strict, -(bc * rel * _dot_nt(kb, kb)), 0.0)
            t_inv = eye_c + p
            pk = p
            for _ in range(int(math.log2(chunk)) - 1):
                pk = _dot_hl(pk, pk)
                t_inv = t_inv + _dot_hl(t_inv, pk)
            eg = jnp.exp(gc)
            rhs = jnp.concatenate([bc * v, (bc * eg) * k], axis=-1)
            sol = _dot_hl(t_inv, rhs)
            u0 = sol[:, :DV]
            w = sol[:, DV:]
            qk = jnp.where(incl, _dot_nt(q, kb) * rel, 0.0)
            g_end = gc[chunk - 1:chunk, :]
            kdec = k * jnp.exp(g_end - gc)
            s = sgd_ref[sq, hd]
            ws = _dot(jnp.concatenate([w, q * eg], axis=0), s)
            u = u0 - ws[:chunk]
            o = ws[chunk:] + _dot(qk, u)
            sgd_ref[sq, hd] = jnp.exp(g_end) * s + _dot_tn(kdec, u)
            ogd_ref[rs, hd * DV:(hd + 1) * DV] = o

    hog = _dot(hb, w1_ref[:, 3 * QK:4 * QK])
    gz = _dot(hb, w2_ref[:, 0:QK])

    def head_norm(o, wn):
        parts = []
        for hd in range(HEADS):
            parts.append(_rms_rows(o[:, hd * DV:(hd + 1) * DV]) * wn)
        return jnp.concatenate(parts, axis=-1)

    o_a = head_norm(ohg_ref[...], hgn_ref[...]) * _silu(hog)
    o_b = head_norm(ogd_ref[...], gdn_ref[...]) * _silu(gz)
    y_a = _dot(o_a, wpa_ref[...])
    y_b = _dot(o_b, wpb_ref[...])
    gate_a = _dot(hb, w2_ref[:, QK:QK + D_MODEL])
    gate_b = _dot(hb, w2_ref[:, QK + D_MODEL:QK + 2 * D_MODEL])
    ym = _sigmoid(gate_a) * y_a + _sigmoid(gate_b) * y_b
    out = _dot(ym, wout_ref[...])
    g1 = _bcast_rows(mod_ref, 2, bt, ts)
    xo_ref[...] = (x + g1 * out).reshape(bt, ts, D_MODEL)


def _const_spec(shape):
    nd = len(shape)
    return pl.BlockSpec(shape, lambda *_: (0,) * nd, pipeline_mode=pl.Buffered(1))


def _mixer(x, mod, mod_row0, states, hg_lb, lw, *, layer, bt, ts, chunk):
    b, t, _ = x.shape
    carry = states is None
    nb = b // bt
    nt = t // ts
    if carry:
        grid = (nb, nt)
        bmap3 = lambda i, j: (i, j, 0)
        smap4 = lambda i, j: (i, 0, 0, 0)
        smap3 = lambda i, j: (i, 0, 0)
        modmap = lambda i, j: (layer, mod_row0 // bt + i, 0, 0)
        sem = ("arbitrary", "arbitrary")
    else:
        assert nt == 1 and ts == chunk
        grid = (nb,)
        bmap3 = lambda i: (i, 0, 0)
        smap4 = lambda i: (i, 0, 0, 0)
        smap3 = lambda i: (i, 0, 0)
        modmap = lambda i: (layer, mod_row0 // bt + i, 0, 0)
        sem = ("arbitrary",)

    in_specs = [pl.BlockSpec((bt, ts, D_MODEL), bmap3),
                pl.BlockSpec((None, bt, N_MOD, D_MODEL), modmap)]
    args = [x, mod]
    if not carry:
        in_specs += [pl.BlockSpec((bt, HEADS, DK, DV), smap4),
                     pl.BlockSpec((bt, HEADS, DK, DV), smap4),
                     pl.BlockSpec((bt, CONV_W - 1, CONV_CH), smap3)]
        args += list(states)
    consts = [hg_lb, lw["norm_mix"], lw["w1"], lw["wab"], lw["w2"], lw["hg_norm"], lw["conv_w"], lw["a_log"],
              lw["dt_bias"], lw["gdn_norm"], lw["w_pa"], lw["w_pb"], lw["w_out"]]
    in_specs += [_const_spec(a.shape) for a in consts]
    args += consts

    out_shape = (jax.ShapeDtypeStruct((b, t, D_MODEL), _F32),
                 jax.ShapeDtypeStruct((b, HEADS, DK, DV), _F32),
                 jax.ShapeDtypeStruct((b, HEADS, DK, DV), _F32),
                 jax.ShapeDtypeStruct((b, CONV_W - 1, CONV_CH), _F32))
    out_specs = (pl.BlockSpec((bt, ts, D_MODEL), bmap3),
                 pl.BlockSpec((bt, HEADS, DK, DV), smap4),
                 pl.BlockSpec((bt, HEADS, DK, DV), smap4),
                 pl.BlockSpec((bt, CONV_W - 1, CONV_CH), smap3))
    r = bt * ts
    return pl.pallas_call(
        functools.partial(_mixer_kernel, layer=layer, bt=bt, ts=ts, chunk=chunk, carry=carry),
        grid=grid, in_specs=in_specs, out_specs=out_specs, out_shape=out_shape,
        scratch_shapes=[pltpu.VMEM((bt, PAD + ts, CONV_CH), _F32),
                        pltpu.VMEM((r, QK), _F32),
                        pltpu.VMEM((r, QK), _F32)],
        compiler_params=pltpu.CompilerParams(dimension_semantics=sem, vmem_limit_bytes=VMEM_LIMIT),
        name=f"mixer_l{layer}_{'prompt' if carry else 'sample'}",
    )(*args)


FF_HALF = D_FF // 2


def _ffn_kernel(x_ref, mod_ref, nffn_ref, wup_ref, wdn_ref, fin_ref, o_ref, act_ref, *, bt, ts, last):
    r = bt * ts
    x = x_ref[...].reshape(r, D_MODEL)
    sh2 = _bcast_rows(mod_ref, 3, bt, ts)
    sc2 = _bcast_rows(mod_ref, 4, bt, ts)
    g2 = _bcast_rows(mod_ref, 5, bt, ts)
    hb = (_rms_rows(x) * nffn_ref[...] * (1.0 + sc2) + sh2).astype(_BF16)
    for j in range(2):
        lo = j * FF_HALF
        gate = jnp.dot(hb, wup_ref[:, lo:lo + FF_HALF], preferred_element_type=_F32)
        up = jnp.dot(hb, wup_ref[:, D_FF + lo:D_FF + lo + FF_HALF], preferred_element_type=_F32)
        act_ref[:, lo:lo + FF_HALF] = (_silu(gate) * up).astype(_BF16)
    out = jnp.dot(act_ref[...], wdn_ref[...], preferred_element_type=_F32)
    xn = x + g2 * out
    if last:
        xn = _rms_rows(xn) * fin_ref[...]
    o_ref[...] = xn.reshape(bt, ts, D_MODEL)


def _ffn(x, mod, mod_row0, lw, final_norm, *, layer, bt, ts, last):
    b, t, _ = x.shape
    grid = (b // bt, t // ts)
    consts = [lw["norm_ffn"], lw["w_up"], lw["w_down"], final_norm]
    r = bt * ts
    return pl.pallas_call(
        functools.partial(_ffn_kernel, bt=bt, ts=ts, last=last),
        grid=grid,
        in_specs=[pl.BlockSpec((bt, ts, D_MODEL), lambda i, j: (i, j, 0)),
                  pl.BlockSpec((None, bt, N_MOD, D_MODEL), lambda i, j: (layer, mod_row0 // bt + i, 0, 0))]
                 + [_const_spec(a.shape) for a in consts],
        out_specs=pl.BlockSpec((bt, ts, D_MODEL), lambda i, j: (i, j, 0)),
        out_shape=jax.ShapeDtypeStruct((b, t, D_MODEL), _F32),
        scratch_shapes=[pltpu.VMEM((r, D_FF), _BF16)],
        compiler_params=pltpu.CompilerParams(dimension_semantics=("arbitrary", "arbitrary"),
                                             vmem_limit_bytes=VMEM_LIMIT),
        name=f"ffn_l{layer}",
    )(x, mod, *consts)


def _layer_weights(l, norm_mix, w_in, hg_norm, conv_w, gdn_a_log, gdn_dt_bias, gdn_norm, w_proj_a, w_proj_b,
                   w_out, norm_ffn, w_up, w_down):
    pad_lanes = lambda v, off: jnp.zeros((1, 128), _F32).at[0, off:off + HEADS].set(v.astype(_F32))
    wl = w_in[l]
    wab = jnp.zeros((D_MODEL, 128), _BF16).at[:, :2 * HEADS].set(wl[:, _C_AB:_C_REST].astype(_BF16))
    return {
        "norm_mix": norm_mix[l].reshape(1, D_MODEL),
        "w1": wl[:, :_C_AB].astype(_BF16),
        "wab": wab,
        "w2": wl[:, _C_REST:].astype(_BF16),
        "hg_norm": hg_norm[l].reshape(1, DV),
        "conv_w": conv_w[l],
        "a_log": pad_lanes(gdn_a_log[l], 0),
        "dt_bias": pad_lanes(gdn_dt_bias[l], 0),
        "gdn_norm": gdn_norm[l].reshape(1, DV),
        "w_pa": w_proj_a[l].astype(_BF16),
        "w_pb": w_proj_b[l].astype(_BF16),
        "w_out": w_out[l].astype(_BF16),
        "norm_ffn": norm_ffn[l].reshape(1, D_MODEL),
        "w_up": w_up[l].astype(_BF16),
        "w_down": w_down[l].astype(_BF16),
    }


def kernel(x_prompt, x_sample, state_hgrn, state_gdn, state_conv, c_prompt, c_sample, w_ada, b_ada, norm_mix,
           w_in, hg_lb, hg_norm, conv_w, gdn_a_log, gdn_dt_bias, gdn_norm, w_proj_a, w_proj_b, w_out, norm_ffn,
           w_up, w_down, final_norm):
    n_prompt = x_prompt.shape[0]
    n_sample, t_sample, _ = x_sample.shape
    c_all = jnp.concatenate([c_prompt, c_sample], axis=0)
    mod = _modulation(c_all, w_ada, b_ada).reshape(DEPTH, n_prompt + n_sample, N_MOD, D_MODEL)
    fin = final_norm.reshape(1, D_MODEL)
    hg_lb = hg_lb.astype(_F32)

    xp, xs = x_prompt, x_sample
    hg_p, gd_p, cv_p, hg_s, gd_s, cv_s = [], [], [], [], [], []
    for l in range(DEPTH):
        lw = _layer_weights(l, norm_mix, w_in, hg_norm, conv_w, gdn_a_log, gdn_dt_bias, gdn_norm, w_proj_a,
                            w_proj_b, w_out, norm_ffn, w_up, w_down)
        last = l == DEPTH - 1
        xp, a, b, c = _mixer(xp, mod, 0, None, hg_lb, lw, layer=l, bt=1, ts=256, chunk=64)
        hg_p.append(a); gd_p.append(b); cv_p.append(c)
        xp = _ffn(xp, mod, 0, lw, fin, layer=l, bt=1, ts=512, last=last)
        xs, a, b, c = _mixer(xs, mod, n_prompt, (state_hgrn[l], state_gdn[l], state_conv[l]), hg_lb, lw,
                             layer=l, bt=8, ts=t_sample, chunk=t_sample)
        hg_s.append(a); gd_s.append(b); cv_s.append(c)
        xs = _ffn(xs, mod, n_prompt, lw, fin, layer=l, bt=8, ts=t_sample, last=last)
    st = jnp.stack
    return (xp, xs, st(hg_p), st(gd_p), st(cv_p), st(hg_s), st(gd_s), st(cv_s))
```

```python
import functools
import math

import jax
import jax.numpy as jnp
from jax import lax
from jax.experimental import pallas as pl
from jax.experimental.pallas import tpu as pltpu

D_MODEL = 1024
DEPTH = 2
HEADS = 4
DK = 128
DV = 128
QK = HEADS * DK
CONV_W = 4
CONV_CH = 3 * QK
D_FF = 2816
N_MOD = 6
EPS = 1e-6
PAD = 8

_C_HG = 0
_C_CONV = 4 * QK
_C_AB = _C_CONV + CONV_CH
_C_REST = _C_AB + 2 * HEADS
D_IN = _C_REST + QK + 2 * D_MODEL

VMEM_LIMIT = 56 * 1024 * 1024

_F32 = jnp.float32
_BF16 = jnp.bfloat16


def _dot(a, b):
    return jnp.dot(a.astype(_BF16), b.astype(_BF16), preferred_element_type=_F32)


def _dot_nt(a, b):
    return lax.dot_general(a.astype(_BF16), b.astype(_BF16), (((1,), (1,)), ((), ())),
                           preferred_element_type=_F32)


def _dot_tn(a, b):
    return lax.dot_general(a.astype(_BF16), b.astype(_BF16), (((0,), (0,)), ((), ())),
                           preferred_element_type=_F32)


def _dot_hl(a, b):
    a_hi = a.astype(_BF16)
    a_lo = (a - a_hi.astype(_F32)).astype(_BF16)
    b_hi = b.astype(_BF16)
    b_lo = (b - b_hi.astype(_F32)).astype(_BF16)
    lhs = jnp.concatenate([a_hi, a_hi, a_lo], axis=1)
    rhs = jnp.concatenate([b_hi, b_lo, b_hi], axis=0)
    return jnp.dot(lhs, rhs, preferred_element_type=_F32)


def _split3(x):
    hi = x.astype(_BF16)
    r1 = x - hi.astype(_F32)
    mid = r1.astype(_BF16)
    lo = (r1 - mid.astype(_F32)).astype(_BF16)
    return hi, mid, lo


def _cumsum_rows(tri, x):
    hi, mid, lo = _split3(x)
    d = lambda p: jnp.dot(tri, p, preferred_element_type=_F32)
    return d(hi) + d(mid) + d(lo)


def _sigmoid(x):
    return 1.0 / (1.0 + jnp.exp(-x))


def _silu(x):
    return x * _sigmoid(x)


def _softplus(x):
    return jnp.maximum(x, 0.0) + jnp.log1p(jnp.exp(-jnp.abs(x)))


def _rms_rows(x):
    return x * lax.rsqrt(jnp.mean(x * x, axis=-1, keepdims=True) + EPS)


def _bcast_rows(ref, row, bt, ts):
    v = ref[:, row:row + 1, :]
    return jnp.broadcast_to(v, (bt, ts, v.shape[-1])).reshape(bt * ts, v.shape[-1])


def _block_row_bcast(x, block, row):
    r, n = x.shape
    x3 = x.reshape(r // block, block, n)
    return jnp.broadcast_to(x3[:, row:row + 1, :], x3.shape).reshape(r, n)


def _level_reference(g, m, rows):
    if 2 * m >= 8:
        return _block_row_bcast(g, 2 * m, m - 1)
    r = g.shape[0]
    pos = rows & (2 * m - 1)
    out = g
    for p in range(2 * m):
        shift = p - (m - 1)
        if shift == 0:
            continue
        out = jnp.where(pos == p, pltpu.roll(g, shift % r, axis=0), out)
    return out


def _mod_kernel(c_ref, w_ref, b_ref, o_ref):
    a = _silu(c_ref[...])
    o_ref[...] = _dot(a, w_ref[...]) + b_ref[...]


def _modulation(c_all, w_ada, b_ada):
    rows = c_all.shape[0]
    tn = 1024
    n = N_MOD * D_MODEL
    return pl.pallas_call(
        _mod_kernel,
        grid=(DEPTH, n // tn),
        in_specs=[
            pl.BlockSpec((rows, D_MODEL), lambda l, j: (0, 0)),
            pl.BlockSpec((None, D_MODEL, tn), lambda l, j: (l, 0, j)),
            pl.BlockSpec((None, 1, tn), lambda l, j: (l, 0, j)),
        ],
        out_specs=pl.BlockSpec((None, rows, tn), lambda l, j: (l, 0, j)),
        out_shape=jax.ShapeDtypeStruct((DEPTH, rows, n), _F32),
        compiler_params=pltpu.CompilerParams(dimension_semantics=("arbitrary", "arbitrary")),
        name="adaln_modulation",
    )(c_all, w_ada, b_ada.reshape(DEPTH, 1, n))


def _mixer_kernel(*refs, layer, bt, ts, chunk, hchunk, carry):
    if carry:
        (x_ref, mod_ref, lb_ref, nmix_ref, w1_ref, wab_ref, w2_ref, hgn_ref, cw_ref, alog_ref, dtb_ref,
         gdn_ref, wpa_ref, wpb_ref, wout_ref,
         xo_ref, shg_ref, sgd_ref, cvo_ref, ext_ref, ohg_ref, ogd_ref) = refs
    else:
        (x_ref, mod_ref, shg0_ref, sgd0_ref, cv0_ref, lb_ref, nmix_ref, w1_ref, wab_ref, w2_ref, hgn_ref,
         cw_ref, alog_ref, dtb_ref, gdn_ref, wpa_ref, wpb_ref, wout_ref,
         xo_ref, shg_ref, sgd_ref, cvo_ref, ext_ref, ohg_ref, ogd_ref) = refs

    r = bt * ts
    n_chunks = r // chunk
    chunks_per_seq = ts // chunk
    n_hchunks = r // hchunk
    hchunks_per_seq = ts // hchunk

    if carry:
        @pl.when(pl.program_id(1) == 0)
        def _():
            shg_ref[...] = jnp.zeros_like(shg_ref)
            sgd_ref[...] = jnp.zeros_like(sgd_ref)
            ext_ref[:, PAD - (CONV_W - 1):PAD, :] = jnp.zeros((bt, CONV_W - 1, CONV_CH), _F32)
    else:
        shg_ref[...] = shg0_ref[...]
        sgd_ref[...] = sgd0_ref[...]
        ext_ref[:, PAD - (CONV_W - 1):PAD, :] = cv0_ref[...]

    x = x_ref[...].reshape(r, D_MODEL)
    sh1 = _bcast_rows(mod_ref, 0, bt, ts)
    sc1 = _bcast_rows(mod_ref, 1, bt, ts)
    h = _rms_rows(x) * nmix_ref[...] * (1.0 + sc1) + sh1
    hb = h.astype(_BF16)

    rows = lax.broadcasted_iota(jnp.int32, (r, 1), 0)
    rr = lax.broadcasted_iota(jnp.int32, (r, r), 0)
    cc = lax.broadcasted_iota(jnp.int32, (r, r), 1)

    def block_tri(size):
        sh = int(math.log2(size))
        return jnp.where(((rr >> sh) == (cc >> sh)) & (cc <= rr), 1.0, 0.0).astype(_BF16)

    tri_g = block_tri(chunk)
    tri_h = tri_g if hchunk == chunk else block_tri(hchunk)
    xor_f = (rr ^ cc).astype(_F32)
    top_bit = (pltpu.bitcast(xor_f, jnp.int32) >> 23) - 127
    code = jnp.where(cc < rr, top_bit, jnp.where(cc == rr, -1, -2))

    ci = lax.broadcasted_iota(jnp.int32, (chunk, chunk), 0)
    cj = lax.broadcasted_iota(jnp.int32, (chunk, chunk), 1)
    incl = cj <= ci
    strict = cj < ci
    eye_c = jnp.where(ci == cj, 1.0, 0.0)

    lbp = lb_ref[...]
    e = jnp.exp(lbp - jnp.max(lbp, axis=0, keepdims=True))
    sm = e / jnp.sum(e, axis=0, keepdims=True)
    cs0 = sm[0:1, :]
    csl = cs0
    for i in range(1, layer + 1):
        csl = csl + sm[i:i + 1, :]
    lb = csl - cs0

    hq = _dot(hb, w1_ref[:, 0:QK]) * DK ** -0.5
    hf = _dot(hb, w1_ref[:, QK:2 * QK])
    hi = _dot(hb, w1_ref[:, 2 * QK:3 * QK])
    log_sig = jnp.minimum(hf, 0.0) - jnp.log1p(jnp.exp(-jnp.abs(hf)))
    la = jnp.log(lb)
    lbv = jnp.log1p(-lb) + log_sig
    log_f = jnp.maximum(la, lbv) + jnp.log1p(jnp.exp(-jnp.abs(la - lbv)))
    k_hg = (1.0 - lb) * (1.0 / (1.0 + jnp.exp(hf)))

    g_hg = _cumsum_rows(tri_h, log_f)
    g_last = _block_row_bcast(g_hg, hchunk, hchunk - 1)
    qg = (hq * jnp.exp(g_hg)).astype(_BF16)
    kd = (k_hg * jnp.exp(g_last - g_hg)).astype(_BF16)
    dec_hg = jnp.exp(g_last)
    hi_b = hi.astype(_BF16)

    levels = []
    m = hchunk // 2
    while m >= 1:
        ref_g = _level_reference(g_hg, m, rows)
        ev = jnp.exp(-jnp.abs(g_hg - ref_g))
        levels.append((int(math.log2(m)), (hq * ev).astype(_BF16), (k_hg * ev).astype(_BF16)))
        m //= 2
    hq_b = hq.astype(_BF16)
    khg_b = k_hg.astype(_BF16)

    for hd in range(HEADS):
        ls = slice(hd * DK, (hd + 1) * DK)
        a = jnp.where(code == -1, _dot_nt(hq_b[:, ls], khg_b[:, ls]), 0.0)
        for (sh, qe, ke) in levels:
            a = jnp.where(code == sh, _dot_nt(qe[:, ls], ke[:, ls]), a)
        o_intra = _dot(a, hi_b[:, ls])
        for c in range(n_hchunks):
            sq = c // hchunks_per_seq
            rs = slice(c * hchunk, (c + 1) * hchunk)
            s = shg_ref[sq, hd]
            ohg_ref[rs, ls] = o_intra[rs] + _dot(qg[rs, ls], s)
            dcol = jnp.transpose(jnp.broadcast_to(dec_hg[c * hchunk:c * hchunk + 1, ls], (DK, DK)))
            shg_ref[sq, hd] = dcol * s + _dot_tn(kd[rs, ls], hi_b[rs, ls])

    u_pre = _dot(hb, w1_ref[:, _C_CONV:_C_CONV + CONV_CH]).reshape(bt, ts, CONV_CH)
    ext_ref[:, PAD:PAD + ts, :] = u_pre
    cw = cw_ref[...]
    base = PAD - (CONV_W - 1)
    y = ext_ref[:, base:base + ts, :] * cw[0:1, :]
    for j in range(1, CONV_W):
        y = y + ext_ref[:, base + j:base + j + ts, :] * cw[j:j + 1, :]
    tail = ext_ref[:, PAD + ts - (CONV_W - 1):PAD + ts, :]
    cvo_ref[...] = tail
    ext_ref[:, PAD - (CONV_W - 1):PAD, :] = tail
    conv = _silu(y).reshape(r, CONV_CH)

    gab = _dot(hb, wab_ref[...])
    log_a = -jnp.exp(alog_ref[...]) * _softplus(gab + dtb_ref[...])
    beta = _sigmoid(gab)
    g_gd = _cumsum_rows(tri_g, log_a)
    g_gd_t = jnp.transpose(g_gd)

    def l2n(z):
        return z * lax.rsqrt(jnp.sum(z * z, axis=-1, keepdims=True) + EPS)

    items = []
    for c in range(n_chunks):
        rs = slice(c * chunk, (c + 1) * chunk)
        for hd in range(HEADS):
            q = l2n(conv[rs, hd * DK:(hd + 1) * DK]) * DK ** -0.5
            k = l2n(conv[rs, QK + hd * DK:QK + (hd + 1) * DK])
            v = conv[rs, 2 * QK + hd * DV:2 * QK + (hd + 1) * DV]
            gc = g_gd[rs, hd:hd + 1]
            gr = g_gd_t[hd:hd + 1, rs]
            bc = beta[rs, HEADS + hd:HEADS + hd + 1]
            rel = jnp.exp(jnp.where(incl, gc - gr, -jnp.inf))
            kb = k.astype(_BF16)
            p = jnp.where(strict, -(bc * rel * _dot_nt(kb, kb)), 0.0)
            eg = jnp.exp(gc)
            g_end = gc[chunk - 1:chunk, :]
            items.append(dict(
                c=c, hd=hd, pk=p, t_inv=eye_c + p,
                rhs=jnp.concatenate([bc * v, (bc * eg) * k], axis=-1),
                qk=jnp.where(incl, _dot_nt(q, kb) * rel, 0.0).astype(_BF16),
                qe=(q * eg).astype(_BF16),
                kdec=(k * jnp.exp(g_end - gc)).astype(_BF16),
                dec=jnp.exp(g_end)))
    for _ in range(int(math.log2(chunk)) - 1):
        for it in items:
            it["pk"] = _dot_hl(it["pk"], it["pk"])
        for it in items:
            it["t_inv"] = it["t_inv"] + _dot_hl(it["t_inv"], it["pk"])
    for it in items:
        sol = _dot_hl(it["t_inv"], it["rhs"])
        it["u0"] = sol[:, :DV]
        it["wq"] = jnp.concatenate([sol[:, DV:].astype(_BF16), it["qe"]], axis=0)

    state = {}
    for it in items:
        c, hd = it["c"], it["hd"]
        sq = c // chunks_per_seq
        s = sgd_ref[sq, hd] if c % chunks_per_seq == 0 else state[hd]
        ws = _dot(it["wq"], s)
        u = (it["u0"] - ws[:chunk]).astype(_BF16)
        ogd_ref[c * chunk:(c + 1) * chunk, hd * DV:(hd + 1) * DV] = ws[chunk:] + _dot(it["qk"], u)
        s = it["dec"] * s + _dot_tn(it["kdec"], u)
        state[hd] = s
        if c % chunks_per_seq == chunks_per_seq - 1:
            sgd_ref[sq, hd] = s

    hog = _dot(hb, w1_ref[:, 3 * QK:4 * QK])
    gz = _dot(hb, w2_ref[:, 0:QK])

    def head_norm(o, wn):
        parts = []
        for hd in range(HEADS):
            parts.append(_rms_rows(o[:, hd * DV:(hd + 1) * DV]) * wn)
        return jnp.concatenate(parts, axis=-1)

    o_a = head_norm(ohg_ref[...], hgn_ref[...]) * _silu(hog)
    o_b = head_norm(ogd_ref[...], gdn_ref[...]) * _silu(gz)
    y_a = _dot(o_a, wpa_ref[...])
    y_b = _dot(o_b, wpb_ref[...])
    gate_a = _dot(hb, w2_ref[:, QK:QK + D_MODEL])
    gate_b = _dot(hb, w2_ref[:, QK + D_MODEL:QK + 2 * D_MODEL])
    ym = _sigmoid(gate_a) * y_a + _sigmoid(gate_b) * y_b
    out = _dot(ym, wout_ref[...])
    g1 = _bcast_rows(mod_ref, 2, bt, ts)
    xo_ref[...] = (x + g1 * out).reshape(bt, ts, D_MODEL)


def _const_spec(shape):
    nd = len(shape)
    return pl.BlockSpec(shape, lambda *_: (0,) * nd, pipeline_mode=pl.Buffered(1))


def _mixer(x, mod, mod_row0, states, hg_lb, lw, *, layer, bt, ts, chunk, hchunk):
    b, t, _ = x.shape
    carry = states is None
    nb = b // bt
    nt = t // ts
    if carry:
        grid = (nb, nt)
        bmap3 = lambda i, j: (i, j, 0)
        smap4 = lambda i, j: (i, 0, 0, 0)
        smap3 = lambda i, j: (i, 0, 0)
        modmap = lambda i, j: (layer, mod_row0 // bt + i, 0, 0)
        sem = ("arbitrary", "arbitrary")
    else:
        assert nt == 1 and ts == chunk
        grid = (nb,)
        bmap3 = lambda i: (i, 0, 0)
        smap4 = lambda i: (i, 0, 0, 0)
        smap3 = lambda i: (i, 0, 0)
        modmap = lambda i: (layer, mod_row0 // bt + i, 0, 0)
        sem = ("arbitrary",)

    in_specs = [pl.BlockSpec((bt, ts, D_MODEL), bmap3),
                pl.BlockSpec((None, bt, N_MOD, D_MODEL), modmap)]
    args = [x, mod]
    if not carry:
        in_specs += [pl.BlockSpec((bt, HEADS, DK, DV), smap4),
                     pl.BlockSpec((bt, HEADS, DK, DV), smap4),
                     pl.BlockSpec((bt, CONV_W - 1, CONV_CH), smap3)]
        args += list(states)
    consts = [hg_lb, lw["norm_mix"], lw["w1"], lw["wab"], lw["w2"], lw["hg_norm"], lw["conv_w"], lw["a_log"],
              lw["dt_bias"], lw["gdn_norm"], lw["w_pa"], lw["w_pb"], lw["w_out"]]
    in_specs += [_const_spec(a.shape) for a in consts]
    args += consts

    out_shape = (jax.ShapeDtypeStruct((b, t, D_MODEL), _F32),
                 jax.ShapeDtypeStruct((b, HEADS, DK, DV), _F32),
                 jax.ShapeDtypeStruct((b, HEADS, DK, DV), _F32),
                 jax.ShapeDtypeStruct((b, CONV_W - 1, CONV_CH), _F32))
    out_specs = (pl.BlockSpec((bt, ts, D_MODEL), bmap3),
                 pl.BlockSpec((bt, HEADS, DK, DV), smap4),
                 pl.BlockSpec((bt, HEADS, DK, DV), smap4),
                 pl.BlockSpec((bt, CONV_W - 1, CONV_CH), smap3))
    r = bt * ts
    return pl.pallas_call(
        functools.partial(_mixer_kernel, layer=layer, bt=bt, ts=ts, chunk=chunk, hchunk=hchunk, carry=carry),
        grid=grid, in_specs=in_specs, out_specs=out_specs, out_shape=out_shape,
        scratch_shapes=[pltpu.VMEM((bt, PAD + ts, CONV_CH), _F32),
                        pltpu.VMEM((r, QK), _F32),
                        pltpu.VMEM((r, QK), _F32)],
        compiler_params=pltpu.CompilerParams(dimension_semantics=sem, vmem_limit_bytes=VMEM_LIMIT),
        name=f"mixer_l{layer}_{'prompt' if carry else 'sample'}",
    )(*args)


FF_HALF = D_FF // 2


def _ffn_kernel(x_ref, mod_ref, nffn_ref, wup_ref, wdn_ref, fin_ref, o_ref, act_ref, *, bt, ts, last):
    r = bt * ts
    x = x_ref[...].reshape(r, D_MODEL)
    sh2 = _bcast_rows(mod_ref, 3, bt, ts)
    sc2 = _bcast_rows(mod_ref, 4, bt, ts)
    g2 = _bcast_rows(mod_ref, 5, bt, ts)
    hb = (_rms_rows(x) * nffn_ref[...] * (1.0 + sc2) + sh2).astype(_BF16)
    for j in range(2):
        lo = j * FF_HALF
        gate = jnp.dot(hb, wup_ref[:, lo:lo + FF_HALF], preferred_element_type=_F32)
        up = jnp.dot(hb, wup_ref[:, D_FF + lo:D_FF + lo + FF_HALF], preferred_element_type=_F32)
        act_ref[:, lo:lo + FF_HALF] = (_silu(gate) * up).astype(_BF16)
    out = jnp.dot(act_ref[...], wdn_ref[...], preferred_element_type=_F32)
    xn = x + g2 * out
    if last:
        xn = _rms_rows(xn) * fin_ref[...]
    o_ref[...] = xn.reshape(bt, ts, D_MODEL)


def _ffn(x, mod, mod_row0, lw, final_norm, *, layer, bt, ts, last):
    b, t, _ = x.shape
    grid = (b // bt, t // ts)
    consts = [lw["norm_ffn"], lw["w_up"], lw["w_down"], final_norm]
    r = bt * ts
    return pl.pallas_call(
        functools.partial(_ffn_kernel, bt=bt, ts=ts, last=last),
        grid=grid,
        in_specs=[pl.BlockSpec((bt, ts, D_MODEL), lambda i, j: (i, j, 0)),
                  pl.BlockSpec((None, bt, N_MOD, D_MODEL), lambda i, j: (layer, mod_row0 // bt + i, 0, 0))]
                 + [_const_spec(a.shape) for a in consts],
        out_specs=pl.BlockSpec((bt, ts, D_MODEL), lambda i, j: (i, j, 0)),
        out_shape=jax.ShapeDtypeStruct((b, t, D_MODEL), _F32),
        scratch_shapes=[pltpu.VMEM((r, D_FF), _BF16)],
        compiler_params=pltpu.CompilerParams(dimension_semantics=("arbitrary", "arbitrary"),
                                             vmem_limit_bytes=VMEM_LIMIT),
        name=f"ffn_l{layer}",
    )(x, mod, *consts)


def _layer_weights(l, norm_mix, w_in, hg_norm, conv_w, gdn_a_log, gdn_dt_bias, gdn_norm, w_proj_a, w_proj_b,
                   w_out, norm_ffn, w_up, w_down):
    pad_lanes = lambda v, off: jnp.zeros((1, 128), _F32).at[0, off:off + HEADS].set(v.astype(_F32))
    wl = w_in[l]
    wab = jnp.zeros((D_MODEL, 128), _BF16).at[:, :2 * HEADS].set(wl[:, _C_AB:_C_REST].astype(_BF16))
    return {
        "norm_mix": norm_mix[l].reshape(1, D_MODEL),
        "w1": wl[:, :_C_AB].astype(_BF16),
        "wab": wab,
        "w2": wl[:, _C_REST:].astype(_BF16),
        "hg_norm": hg_norm[l].reshape(1, DV),
        "conv_w": conv_w[l],
        "a_log": pad_lanes(gdn_a_log[l], 0),
        "dt_bias": pad_lanes(gdn_dt_bias[l], 0),
        "gdn_norm": gdn_norm[l].reshape(1, DV),
        "w_pa": w_proj_a[l].astype(_BF16),
        "w_pb": w_proj_b[l].astype(_BF16),
        "w_out": w_out[l].astype(_BF16),
        "norm_ffn": norm_ffn[l].reshape(1, D_MODEL),
        "w_up": w_up[l].astype(_BF16),
        "w_down": w_down[l].astype(_BF16),
    }


def kernel(x_prompt, x_sample, state_hgrn, state_gdn, state_conv, c_prompt, c_sample, w_ada, b_ada, norm_mix,
           w_in, hg_lb, hg_norm, conv_w, gdn_a_log, gdn_dt_bias, gdn_norm, w_proj_a, w_proj_b, w_out, norm_ffn,
           w_up, w_down, final_norm):
    n_prompt = x_prompt.shape[0]
    n_sample, t_sample, _ = x_sample.shape
    c_all = jnp.concatenate([c_prompt, c_sample], axis=0)
    mod = _modulation(c_all, w_ada, b_ada).reshape(DEPTH, n_prompt + n_sample, N_MOD, D_MODEL)
    fin = final_norm.reshape(1, D_MODEL)
    hg_lb = hg_lb.astype(_F32)

    xp, xs = x_prompt, x_sample
    hg_p, gd_p, cv_p, hg_s, gd_s, cv_s = [], [], [], [], [], []
    for l in range(DEPTH):
        lw = _layer_weights(l, norm_mix, w_in, hg_norm, conv_w, gdn_a_log, gdn_dt_bias, gdn_norm, w_proj_a,
                            w_proj_b, w_out, norm_ffn, w_up, w_down)
        last = l == DEPTH - 1
        xp, a, b, c = _mixer(xp, mod, 0, None, hg_lb, lw, layer=l, bt=1, ts=256, chunk=64, hchunk=256)
        hg_p.append(a); gd_p.append(b); cv_p.append(c)
        xp = _ffn(xp, mod, 0, lw, fin, layer=l, bt=1, ts=512, last=last)
        xs, a, b, c = _mixer(xs, mod, n_prompt, (state_hgrn[l], state_gdn[l], state_conv[l]), hg_lb, lw,
                             layer=l, bt=8, ts=t_sample, chunk=t_sample, hchunk=t_sample)
        hg_s.append(a); gd_s.append(b); cv_s.append(c)
        xs = _ffn(xs, mod, n_prompt, lw, fin, layer=l, bt=8, ts=t_sample, last=last)
    st = jnp.stack
    return (xp, xs, st(hg_p), st(gd_p), st(cv_p), st(hg_s), st(gd_s), st(cv_s))
```

```python
import functools
import math

import jax
import jax.numpy as jnp
import numpy as np
from jax import lax
from jax.experimental import pallas as pl
from jax.experimental.pallas import tpu as pltpu

D_MODEL = 1024
DEPTH = 2
HEADS = 4
DK = 128
DV = 128
QK = HEADS * DK
CONV_W = 4
CONV_CH = 3 * QK
D_FF = 2816
N_MOD = 6
EPS = 1e-6
LOG2E = 1.4426950408889634
PAD = 8
MXU_COLS = 256

_C_HG = 0
_C_CONV = 4 * QK
_C_AB = _C_CONV + CONV_CH
_C_REST = _C_AB + 2 * HEADS
D_IN = _C_REST + QK + 2 * D_MODEL

VMEM_LIMIT = 56 * 1024 * 1024

_F32 = jnp.float32
_BF16 = jnp.bfloat16


def _dot(a, b):
    return jnp.dot(a.astype(_BF16), b.astype(_BF16), preferred_element_type=_F32)


def _dot_nt(a, b):
    return lax.dot_general(a.astype(_BF16), b.astype(_BF16), (((1,), (1,)), ((), ())),
                           preferred_element_type=_F32)


def _dot_tn(a, b):
    return lax.dot_general(a.astype(_BF16), b.astype(_BF16), (((0,), (0,)), ((), ())),
                           preferred_element_type=_F32)


def _dot_hl(a, b):
    a_hi = a.astype(_BF16)
    a_lo = (a - a_hi.astype(_F32)).astype(_BF16)
    b_hi = b.astype(_BF16)
    b_lo = (b - b_hi.astype(_F32)).astype(_BF16)
    lhs = jnp.concatenate([a_hi, a_hi, a_lo], axis=1)
    rhs = jnp.concatenate([b_hi, b_lo, b_hi], axis=0)
    return jnp.dot(lhs, rhs, preferred_element_type=_F32)


def _split3(x):
    hi = x.astype(_BF16)
    r1 = x - hi.astype(_F32)
    mid = r1.astype(_BF16)
    lo = (r1 - mid.astype(_F32)).astype(_BF16)
    return hi, mid, lo


def _cumsum_rows(tri, x):
    hi, mid, lo = _split3(x)
    d = lambda p: jnp.dot(tri, p, preferred_element_type=_F32)
    return d(hi) + d(mid) + d(lo)


def _sigmoid(x):
    return 1.0 / (1.0 + jnp.exp(-x))


def _silu(x):
    return x * _sigmoid(x)


def _softplus(x):
    return jnp.maximum(x, 0.0) + jnp.log1p(jnp.exp(-jnp.abs(x)))


def _rms_rows(x):
    return x * lax.rsqrt(jnp.mean(x * x, axis=-1, keepdims=True) + EPS)


def _bcast_rows(ref, row, bt, ts):
    v = ref[:, row:row + 1, :]
    return jnp.broadcast_to(v, (bt, ts, v.shape[-1])).reshape(bt * ts, v.shape[-1])


def _block_row_bcast(x, block, row):
    r, n = x.shape
    x3 = x.reshape(r // block, block, n)
    return jnp.broadcast_to(x3[:, row:row + 1, :], x3.shape).reshape(r, n)


def _level_reference(g, m, rows):
    if 2 * m >= 8:
        return _block_row_bcast(g, 2 * m, m - 1)
    r = g.shape[0]
    pos = rows & (2 * m - 1)
    out = g
    for p in range(2 * m):
        shift = p - (m - 1)
        if shift == 0:
            continue
        out = jnp.where(pos == p, pltpu.roll(g, shift % r, axis=0), out)
    return out


def _chained(*gens):
    for g in gens:
        yield from g


def _alternating(a, b):
    live = [a, b]
    while live:
        for g in list(live):
            try:
                next(g)
                yield
            except StopIteration:
                live.remove(g)


def _run_interleaved(chain_steps, n_chain, fill_steps, n_fill):
    per = -(-n_fill // n_chain)
    for _ in chain_steps:
        for _ in range(per):
            next(fill_steps, None)
    for _ in fill_steps:
        pass


def _mod_kernel(c_ref, w_ref, b_ref, o_ref):
    a = _silu(c_ref[...])
    o_ref[...] = _dot(a, w_ref[...]) + b_ref[...]


def _modulation(c_all, w_ada, b_ada):
    rows = c_all.shape[0]
    tn = 1024
    n = N_MOD * D_MODEL
    return pl.pallas_call(
        _mod_kernel,
        grid=(DEPTH, n // tn),
        in_specs=[
            pl.BlockSpec((rows, D_MODEL), lambda l, j: (0, 0)),
            pl.BlockSpec((None, D_MODEL, tn), lambda l, j: (l, 0, j)),
            pl.BlockSpec((None, 1, tn), lambda l, j: (l, 0, j)),
        ],
        out_specs=pl.BlockSpec((None, rows, tn), lambda l, j: (l, 0, j)),
        out_shape=jax.ShapeDtypeStruct((DEPTH, rows, n), _F32),
        compiler_params=pltpu.CompilerParams(dimension_semantics=("arbitrary", "arbitrary")),
        name="adaln_modulation",
    )(c_all, w_ada, b_ada.reshape(DEPTH, 1, n))


def _mixer_kernel(*refs, layer, bt, ts, chunk, hchunk, carry):
    if carry:
        (x_ref, mod_ref, lb_ref, nmix_ref, w1_ref, wab_ref, w2_ref, hgn_ref, cw_ref, alog_ref, dtb_ref,
         gdn_ref, wpa_ref, wpb_ref, wout_ref, trih_ref, trig_ref, code_ref, bd_ref, bdk_ref,
         xo_ref, shg_ref, sgd_ref, cvo_ref, ext_ref) = refs
    else:
        (x_ref, mod_ref, shg0_ref, sgd0_ref, cv0_ref, lb_ref, nmix_ref, w1_ref, wab_ref, w2_ref, hgn_ref,
         cw_ref, alog_ref, dtb_ref, gdn_ref, wpa_ref, wpb_ref, wout_ref, trih_ref, trig_ref, code_ref, bd_ref,
         bdk_ref, xo_ref, shg_ref, sgd_ref, cvo_ref, ext_ref) = refs

    r = bt * ts
    n_seq = bt
    n_chunks = r // chunk
    chunks_per_seq = ts // chunk
    n_hchunks = r // hchunk
    hchunks_per_seq = ts // hchunk

    if carry:
        @pl.when(pl.program_id(1) == 0)
        def _():
            shg_ref[...] = jnp.zeros_like(shg_ref)
            sgd_ref[...] = jnp.zeros_like(sgd_ref)
            ext_ref[:, PAD - (CONV_W - 1):PAD, :] = jnp.zeros((bt, CONV_W - 1, CONV_CH), _F32)
    else:
        shg_ref[...] = shg0_ref[...]
        sgd_ref[...] = sgd0_ref[...]
        ext_ref[:, PAD - (CONV_W - 1):PAD, :] = cv0_ref[...]

    x = x_ref[...].reshape(r, D_MODEL)
    sh1 = _bcast_rows(mod_ref, 0, bt, ts)
    sc1 = _bcast_rows(mod_ref, 1, bt, ts)
    h = _rms_rows(x) * nmix_ref[...] * (1.0 + sc1) + sh1
    hb = h.astype(_BF16)

    rows = lax.broadcasted_iota(jnp.int32, (r, 1), 0)
    tri_h = trih_ref[...]
    tri_g = trig_ref[...]
    code = code_ref[...]

    lbp = lb_ref[...]
    e = jnp.exp(lbp - jnp.max(lbp, axis=0, keepdims=True))
    sm = e / jnp.sum(e, axis=0, keepdims=True)
    cs0 = sm[0:1, :]
    csl = cs0
    for i in range(1, layer + 1):
        csl = csl + sm[i:i + 1, :]
    lb = csl - cs0

    hq = _dot(hb, w1_ref[:, 0:QK]) * DK ** -0.5
    hf = _dot(hb, w1_ref[:, QK:2 * QK])
    hi = _dot(hb, w1_ref[:, 2 * QK:3 * QK])
    e_hf = jnp.exp(-jnp.abs(hf))
    log_sig = jnp.minimum(hf, 0.0) - jnp.log(1.0 + e_hf)
    sig_neg = jnp.where(hf > 0.0, e_hf, 1.0) / (1.0 + e_hf)
    la = jnp.log(lb)
    lbv = jnp.log1p(-lb) + log_sig
    log_f = jnp.maximum(la, lbv) + jnp.log(1.0 + jnp.exp(-jnp.abs(la - lbv)))
    k_hg = (1.0 - lb) * sig_neg

    g_hg = _cumsum_rows(tri_h, log_f)
    g_last = _block_row_bcast(g_hg, hchunk, hchunk - 1)
    qg = (hq * jnp.exp(g_hg)).astype(_BF16)
    kd = (k_hg * jnp.exp(g_last - g_hg)).astype(_BF16)
    dec_hg = jnp.exp(g_last)
    hi_b = hi.astype(_BF16)

    levels = []
    g2_hg = g_hg * LOG2E

    def level_prep(m):
        ev = jnp.exp2(-jnp.abs(g2_hg - _level_reference(g2_hg, m, rows)))
        levels.append((int(math.log2(m)), (hq * ev).astype(_BF16), (k_hg * ev).astype(_BF16)))

    level_sizes = [hchunk >> (i + 1) for i in range(int(math.log2(hchunk)))]
    hq_b = hq.astype(_BF16)
    khg_b = k_hg.astype(_BF16)

    s_hg = [[shg_ref[sq, hd] for hd in range(HEADS)] for sq in range(n_seq)]
    s_gd = [[sgd_ref[sq, hd] for hd in range(HEADS)] for sq in range(n_seq)]
    o_hg = [None] * HEADS
    o_gd = [[None] * n_chunks for _ in range(HEADS)]

    def hgrn_head_steps(hd):
        ls = slice(hd * DK, (hd + 1) * DK)
        outs = []
        for c in range(n_hchunks):
            sq = c // hchunks_per_seq
            rs = slice(c * hchunk, (c + 1) * hchunk)
            a = jnp.where(code == -1, _dot_nt(hq_b[rs, ls], khg_b[rs, ls]), 0.0)
            yield
            for (sh, qe, ke) in levels:
                a = jnp.where(code == sh, _dot_nt(qe[rs, ls], ke[rs, ls]), a)
                yield
            s = s_hg[sq][hd]
            outs.append(_dot(a, hi_b[rs, ls]) + _dot(qg[rs, ls], s))
            dcol = jnp.transpose(jnp.broadcast_to(dec_hg[c * hchunk:c * hchunk + 1, ls], (DK, DK)))
            s_hg[sq][hd] = dcol * s + _dot_tn(kd[rs, ls], hi_b[rs, ls])
            yield
        o_hg[hd] = outs[0] if len(outs) == 1 else jnp.concatenate(outs, axis=0)

    u_pre = _dot(hb, w1_ref[:, _C_CONV:_C_CONV + CONV_CH]).reshape(bt, ts, CONV_CH)
    ext_ref[:, PAD:PAD + ts, :] = u_pre
    cw = cw_ref[...]
    base = PAD - (CONV_W - 1)
    y = ext_ref[:, base:base + ts, :] * cw[0:1, :]
    for j in range(1, CONV_W):
        y = y + ext_ref[:, base + j:base + j + ts, :] * cw[j:j + 1, :]
    tail = ext_ref[:, PAD + ts - (CONV_W - 1):PAD + ts, :]
    cvo_ref[...] = tail
    ext_ref[:, PAD - (CONV_W - 1):PAD, :] = tail
    conv = _silu(y).reshape(r, CONV_CH)

    gab = _dot(hb, wab_ref[...])
    log_a = -jnp.exp(alog_ref[...]) * _softplus(gab + dtb_ref[...])
    beta = _sigmoid(gab)
    g_gd = _cumsum_rows(tri_g, log_a)
    g_gd_t = jnp.transpose(g_gd)

    def l2n_heads(z):
        parts = []
        for hd in range(HEADS):
            zh = z[:, hd * DK:(hd + 1) * DK]
            parts.append(zh * lax.rsqrt(jnp.sum(zh * zh, axis=-1, keepdims=True) + EPS))
        return jnp.concatenate(parts, axis=-1)

    qn = l2n_heads(conv[:, 0:QK]) * DK ** -0.5
    kn = l2n_heads(conv[:, QK:2 * QK])
    vv = conv[:, 2 * QK:]
    qn_b = qn.astype(_BF16)
    kn_b = kn.astype(_BF16)

    wide = HEADS * chunk
    lsh = int(math.log2(chunk))
    w_lane = lax.broadcasted_iota(jnp.int32, (chunk, wide), 1)
    w_row = lax.broadcasted_iota(jnp.int32, (chunk, wide), 0)
    w_head = w_lane >> lsh
    w_pos = w_lane & (chunk - 1)
    incl_w = w_pos <= w_row
    strict_w = w_pos < w_row
    eye_w = jnp.where(w_pos == w_row, 1.0, 0.0)
    bd_b = bd_ref[...]
    bdk_b = bdk_ref[...]

    def block_diag(xb):
        return jnp.concatenate([xb] * HEADS, axis=0) * bd_b

    def head_cols(arr, col0, rs):
        out = jnp.broadcast_to(arr[rs, col0:col0 + 1], (chunk, wide))
        for hd in range(1, HEADS):
            out = jnp.where(w_head == hd, arr[rs, col0 + hd:col0 + hd + 1], out)
        return out

    items = []
    for c in range(n_chunks):
        rs = slice(c * chunk, (c + 1) * chunk)
        k_c = kn_b[rs]
        bdk = jnp.concatenate([k_c] * HEADS, axis=0) * bdk_b
        kq = _dot_nt(jnp.concatenate([k_c, qn_b[rs]], axis=0), bdk)
        gc_w = head_cols(g_gd, 0, rs)
        bc_w = head_cols(beta, HEADS, rs)
        gr_w = jnp.concatenate([g_gd_t[hd:hd + 1, rs] for hd in range(HEADS)], axis=1)
        rel_w = jnp.exp(jnp.where(incl_w, gc_w - gr_w, -jnp.inf))
        p_w = jnp.where(strict_w, -(bc_w * rel_w * kq[:chunk]), 0.0)
        pkb = p_w.astype(_BF16)
        items.append(dict(c=c, rs=rs, p=p_w, pkb=pkb, bd=block_diag(pkb), t=eye_w + p_w,
                          qk=jnp.where(incl_w, kq[chunk:] * rel_w, 0.0).astype(_BF16)))

    def wy_chain():
        for _ in range(lsh - 1):
            for it in items:
                it["pkb"] = _dot(it["pkb"], it["bd"]).astype(_BF16)
                it["bd"] = block_diag(it["pkb"])
            yield
            for it in items:
                it["t"] = it["t"] + _dot(it["t"], it["bd"])
            yield

    heads = []

    def solve_chain():
        for it in items:
            rs = it["rs"]
            tb = it["t"].astype(_BF16)
            a_w = eye_w - it["p"]
            for hd in range(HEADS):
                ls = slice(hd * DK, (hd + 1) * DK)
                ws_ = slice(hd * chunk, (hd + 1) * chunk)
                gc = g_gd[rs, hd:hd + 1]
                bc = beta[rs, HEADS + hd:HEADS + hd + 1]
                eg = jnp.exp(gc)
                g_end = gc[chunk - 1:chunk, :]
                k = kn[rs, ls]
                heads.append(dict(
                    c=it["c"], hd=hd, x0=tb[:, ws_], a=a_w[:, ws_],
                    rhs=jnp.concatenate([bc * vv[rs, ls], (bc * eg) * k], axis=-1),
                    qk=it["qk"][:, ws_],
                    qe=(qn[rs, ls] * eg).astype(_BF16),
                    kdec=(k * jnp.exp(g_end - gc)).astype(_BF16),
                    dec=jnp.exp(g_end)))
        for it in heads:
            it["sol0"] = _dot(it["x0"], it["rhs"])
        yield
        for it in heads:
            it["resid"] = it["rhs"] - _dot_hl(it["a"], it["sol0"])
        yield
        for it in heads:
            sol = it["sol0"] + _dot(it["x0"], it["resid"])
            it["u0"] = sol[:, :DV]
            it["wq"] = jnp.concatenate([sol[:, DV:].astype(_BF16), it["qe"]], axis=0)
        yield

    def gdn_chain():
        for j in range(chunks_per_seq):
            cs = [sq * chunks_per_seq + j for sq in range(n_seq)]
            its = [it for c in cs for it in heads[c * HEADS:(c + 1) * HEADS]]
            wss = []
            for it in its:
                sq = it["c"] // chunks_per_seq
                wss.append(_dot(it["wq"], s_gd[sq][it["hd"]]))
            yield
            for it, ws in zip(its, wss):
                sq = it["c"] // chunks_per_seq
                hd = it["hd"]
                u = (it["u0"] - ws[:chunk]).astype(_BF16)
                o_gd[hd][it["c"]] = ws[chunk:] + _dot(it["qk"], u)
                s_gd[sq][hd] = it["dec"] * s_gd[sq][hd] + _dot_tn(it["kdec"], u)
            yield

    side = {}

    def side_steps():
        for name, w_ref, lo, hi_ in (("hog", w1_ref, 3 * QK, 4 * QK), ("gz", w2_ref, 0, QK),
                                     ("gate_a", w2_ref, QK, QK + D_MODEL),
                                     ("gate_b", w2_ref, QK + D_MODEL, QK + 2 * D_MODEL)):
            for c0 in range(lo, hi_, MXU_COLS):
                side.setdefault(name, []).append(_dot(hb, w_ref[:, c0:c0 + MXU_COLS]))
                yield

    def level_steps():
        for m in level_sizes:
            level_prep(m)
            yield

    n_head = n_hchunks * (len(level_sizes) + 2)
    n_side = (2 * QK + 2 * D_MODEL) // MXU_COLS
    _run_interleaved(wy_chain(), 2 * (lsh - 1),
                     _chained(_alternating(level_steps(), side_steps()), hgrn_head_steps(0)),
                     len(level_sizes) + n_side + n_head)
    _run_interleaved(_chained(solve_chain(), gdn_chain()), 3 + 2 * chunks_per_seq,
                     _chained(*[hgrn_head_steps(hd) for hd in range(1, HEADS)]), (HEADS - 1) * n_head)
    side = {k: jnp.concatenate(v, axis=-1) for k, v in side.items()}

    def head_norm(o, wn):
        parts = []
        for hd in range(HEADS):
            parts.append(_rms_rows(o[:, hd * DV:(hd + 1) * DV]) * wn)
        return jnp.concatenate(parts, axis=-1)

    for sq in range(n_seq):
        for hd in range(HEADS):
            shg_ref[sq, hd] = s_hg[sq][hd]
            sgd_ref[sq, hd] = s_gd[sq][hd]
    o_hg_all = jnp.concatenate(o_hg, axis=-1)
    o_gd_all = jnp.concatenate([jnp.concatenate(o_gd[hd], axis=0) for hd in range(HEADS)], axis=-1)
    o_a = head_norm(o_hg_all, hgn_ref[...]) * _silu(side["hog"])
    o_b = head_norm(o_gd_all, gdn_ref[...]) * _silu(side["gz"])
    y_a = _dot(o_a, wpa_ref[...])
    y_b = _dot(o_b, wpb_ref[...])
    ym = _sigmoid(side["gate_a"]) * y_a + _sigmoid(side["gate_b"]) * y_b
    out = _dot(ym, wout_ref[...])
    g1 = _bcast_rows(mod_ref, 2, bt, ts)
    xo_ref[...] = (x + g1 * out).reshape(bt, ts, D_MODEL)


def _mixer_constants(r, chunk, hchunk):
    t = np.arange(r)[:, None]
    s = np.arange(r)[None, :]

    def block_tri(size):
        return ((t // size == s // size) & (s <= t)).astype(np.float32)

    top_bit = np.floor(np.log2(np.maximum(t ^ s, 1))).astype(np.int32)
    code = np.where(s < t, top_bit, np.where(s == t, -1, -2)).astype(np.int32)[:hchunk, :hchunk]
    wide = HEADS * chunk
    wr = np.arange(wide)[:, None] // chunk
    bd = (wr == np.arange(wide)[None, :] // chunk).astype(np.float32)
    bdk = (wr == np.arange(QK)[None, :] // DK).astype(np.float32)
    return [jnp.asarray(block_tri(hchunk), _BF16), jnp.asarray(block_tri(chunk), _BF16), jnp.asarray(code),
            jnp.asarray(bd, _BF16), jnp.asarray(bdk, _BF16)]


def _const_spec(shape):
    nd = len(shape)
    return pl.BlockSpec(shape, lambda *_: (0,) * nd, pipeline_mode=pl.Buffered(1))


def _mixer(x, mod, mod_row0, states, hg_lb, lw, *, layer, bt, ts, chunk, hchunk):
    b, t, _ = x.shape
    carry = states is None
    nb = b // bt
    nt = t // ts
    if carry:
        grid = (nb, nt)
        bmap3 = lambda i, j: (i, j, 0)
        smap4 = lambda i, j: (i, 0, 0, 0)
        smap3 = lambda i, j: (i, 0, 0)
        modmap = lambda i, j: (layer, mod_row0 // bt + i, 0, 0)
        sem = ("arbitrary", "arbitrary")
    else:
        assert nt == 1 and ts == chunk
        grid = (nb,)
        bmap3 = lambda i: (i, 0, 0)
        smap4 = lambda i: (i, 0, 0, 0)
        smap3 = lambda i: (i, 0, 0)
        modmap = lambda i: (layer, mod_row0 // bt + i, 0, 0)
        sem = ("arbitrary",)

    in_specs = [pl.BlockSpec((bt, ts, D_MODEL), bmap3),
                pl.BlockSpec((None, bt, N_MOD, D_MODEL), modmap)]
    args = [x, mod]
    if not carry:
        in_specs += [pl.BlockSpec((bt, HEADS, DK, DV), smap4),
                     pl.BlockSpec((bt, HEADS, DK, DV), smap4),
                     pl.BlockSpec((bt, CONV_W - 1, CONV_CH), smap3)]
        args += list(states)
    consts = [hg_lb, lw["norm_mix"], lw["w1"], lw["wab"], lw["w2"], lw["hg_norm"], lw["conv_w"], lw["a_log"],
              lw["dt_bias"], lw["gdn_norm"], lw["w_pa"], lw["w_pb"], lw["w_out"]]
    consts += _mixer_constants(bt * ts, chunk, hchunk)
    in_specs += [_const_spec(a.shape) for a in consts]
    args += consts

    out_shape = (jax.ShapeDtypeStruct((b, t, D_MODEL), _F32),
                 jax.ShapeDtypeStruct((b, HEADS, DK, DV), _F32),
                 jax.ShapeDtypeStruct((b, HEADS, DK, DV), _F32),
                 jax.ShapeDtypeStruct((b, CONV_W - 1, CONV_CH), _F32))
    out_specs = (pl.BlockSpec((bt, ts, D_MODEL), bmap3),
                 pl.BlockSpec((bt, HEADS, DK, DV), smap4),
                 pl.BlockSpec((bt, HEADS, DK, DV), smap4),
                 pl.BlockSpec((bt, CONV_W - 1, CONV_CH), smap3))
    return pl.pallas_call(
        functools.partial(_mixer_kernel, layer=layer, bt=bt, ts=ts, chunk=chunk, hchunk=hchunk, carry=carry),
        grid=grid, in_specs=in_specs, out_specs=out_specs, out_shape=out_shape,
        scratch_shapes=[pltpu.VMEM((bt, PAD + ts, CONV_CH), _F32)],
        compiler_params=pltpu.CompilerParams(dimension_semantics=sem, vmem_limit_bytes=VMEM_LIMIT),
        name=f"mixer_l{layer}_{'prompt' if carry else 'sample'}",
    )(*args)


FF_HALF = D_FF // 2


def _ffn_kernel(x_ref, mod_ref, nffn_ref, wup_ref, wdn_ref, fin_ref, o_ref, act_ref, *, bt, ts, last):
    r = bt * ts
    x = x_ref[...].reshape(r, D_MODEL)
    sh2 = _bcast_rows(mod_ref, 3, bt, ts)
    sc2 = _bcast_rows(mod_ref, 4, bt, ts)
    g2 = _bcast_rows(mod_ref, 5, bt, ts)
    hb = (_rms_rows(x) * nffn_ref[...] * (1.0 + sc2) + sh2).astype(_BF16)
    for j in range(2):
        lo = j * FF_HALF
        gate = jnp.dot(hb, wup_ref[:, lo:lo + FF_HALF], preferred_element_type=_F32)
        up = jnp.dot(hb, wup_ref[:, D_FF + lo:D_FF + lo + FF_HALF], preferred_element_type=_F32)
        act_ref[:, lo:lo + FF_HALF] = (_silu(gate) * up).astype(_BF16)
    out = jnp.dot(act_ref[...], wdn_ref[...], preferred_element_type=_F32)
    xn = x + g2 * out
    if last:
        xn = _rms_rows(xn) * fin_ref[...]
    o_ref[...] = xn.reshape(bt, ts, D_MODEL)


def _ffn(x, mod, mod_row0, lw, final_norm, *, layer, bt, ts, last):
    b, t, _ = x.shape
    grid = (b // bt, t // ts)
    consts = [lw["norm_ffn"], lw["w_up"], lw["w_down"], final_norm]
    r = bt * ts
    return pl.pallas_call(
        functools.partial(_ffn_kernel, bt=bt, ts=ts, last=last),
        grid=grid,
        in_specs=[pl.BlockSpec((bt, ts, D_MODEL), lambda i, j: (i, j, 0)),
                  pl.BlockSpec((None, bt, N_MOD, D_MODEL), lambda i, j: (layer, mod_row0 // bt + i, 0, 0))]
                 + [_const_spec(a.shape) for a in consts],
        out_specs=pl.BlockSpec((bt, ts, D_MODEL), lambda i, j: (i, j, 0)),
        out_shape=jax.ShapeDtypeStruct((b, t, D_MODEL), _F32),
        scratch_shapes=[pltpu.VMEM((r, D_FF), _BF16)],
        compiler_params=pltpu.CompilerParams(dimension_semantics=("arbitrary", "arbitrary"),
                                             vmem_limit_bytes=VMEM_LIMIT),
        name=f"ffn_l{layer}",
    )(x, mod, *consts)


def _layer_weights(l, norm_mix, w_in, hg_norm, conv_w, gdn_a_log, gdn_dt_bias, gdn_norm, w_proj_a, w_proj_b,
                   w_out, norm_ffn, w_up, w_down):
    pad_lanes = lambda v, off: jnp.zeros((1, 128), _F32).at[0, off:off + HEADS].set(v.astype(_F32))
    wl = w_in[l]
    wab = jnp.zeros((D_MODEL, 128), _BF16).at[:, :2 * HEADS].set(wl[:, _C_AB:_C_REST].astype(_BF16))
    return {
        "norm_mix": norm_mix[l].reshape(1, D_MODEL),
        "w1": wl[:, :_C_AB].astype(_BF16),
        "wab": wab,
        "w2": wl[:, _C_REST:].astype(_BF16),
        "hg_norm": hg_norm[l].reshape(1, DV),
        "conv_w": conv_w[l],
        "a_log": pad_lanes(gdn_a_log[l], 0),
        "dt_bias": pad_lanes(gdn_dt_bias[l], 0),
        "gdn_norm": gdn_norm[l].reshape(1, DV),
        "w_pa": w_proj_a[l].astype(_BF16),
        "w_pb": w_proj_b[l].astype(_BF16),
        "w_out": w_out[l].astype(_BF16),
        "norm_ffn": norm_ffn[l].reshape(1, D_MODEL),
        "w_up": w_up[l].astype(_BF16),
        "w_down": w_down[l].astype(_BF16),
    }


def kernel(x_prompt, x_sample, state_hgrn, state_gdn, state_conv, c_prompt, c_sample, w_ada, b_ada, norm_mix,
           w_in, hg_lb, hg_norm, conv_w, gdn_a_log, gdn_dt_bias, gdn_norm, w_proj_a, w_proj_b, w_out, norm_ffn,
           w_up, w_down, final_norm):
    n_prompt = x_prompt.shape[0]
    n_sample, t_sample, _ = x_sample.shape
    c_all = jnp.concatenate([c_prompt, c_sample], axis=0)
    mod = _modulation(c_all, w_ada, b_ada).reshape(DEPTH, n_prompt + n_sample, N_MOD, D_MODEL)
    fin = final_norm.reshape(1, D_MODEL)
    hg_lb = hg_lb.astype(_F32)

    xp, xs = x_prompt, x_sample
    hg_p, gd_p, cv_p, hg_s, gd_s, cv_s = [], [], [], [], [], []
    for l in range(DEPTH):
        lw = _layer_weights(l, norm_mix, w_in, hg_norm, conv_w, gdn_a_log, gdn_dt_bias, gdn_norm, w_proj_a,
                            w_proj_b, w_out, norm_ffn, w_up, w_down)
        last = l == DEPTH - 1
        xp, a, b, c = _mixer(xp, mod, 0, None, hg_lb, lw, layer=l, bt=1, ts=256, chunk=64, hchunk=256)
        hg_p.append(a); gd_p.append(b); cv_p.append(c)
        xp = _ffn(xp, mod, 0, lw, fin, layer=l, bt=1, ts=512, last=last)
        xs, a, b, c = _mixer(xs, mod, n_prompt, (state_hgrn[l], state_gdn[l], state_conv[l]), hg_lb, lw,
                             layer=l, bt=8, ts=t_sample, chunk=t_sample, hchunk=t_sample)
        hg_s.append(a); gd_s.append(b); cv_s.append(c)
        xs = _ffn(xs, mod, n_prompt, lw, fin, layer=l, bt=8, ts=t_sample, last=last)
    st = jnp.stack
    return (xp, xs, st(hg_p), st(gd_p), st(cv_p), st(hg_s), st(gd_s), st(cv_s))
```

```python
import functools
import math

import jax
import jax.numpy as jnp
import numpy as np
from jax import lax
from jax.experimental import pallas as pl
from jax.experimental.pallas import tpu as pltpu

D_MODEL = 1024
DEPTH = 2
HEADS = 4
DK = 128
DV = 128
QK = HEADS * DK
CONV_W = 4
CONV_CH = 3 * QK
D_FF = 2816
N_MOD = 6
EPS = 1e-6
LOG2E = 1.4426950408889634
PAD = 8
MXU_COLS = 256

_C_HG = 0
_C_CONV = 4 * QK
_C_AB = _C_CONV + CONV_CH
_C_REST = _C_AB + 2 * HEADS
D_IN = _C_REST + QK + 2 * D_MODEL

VMEM_LIMIT = 56 * 1024 * 1024
FUSED_VMEM_LIMIT = 62 * 1024 * 1024

_F32 = jnp.float32
_BF16 = jnp.bfloat16


def _dot(a, b):
    return jnp.dot(a.astype(_BF16), b.astype(_BF16), preferred_element_type=_F32)


def _dot_nt(a, b):
    return lax.dot_general(a.astype(_BF16), b.astype(_BF16), (((1,), (1,)), ((), ())),
                           preferred_element_type=_F32)


def _dot_tn(a, b):
    return lax.dot_general(a.astype(_BF16), b.astype(_BF16), (((0,), (0,)), ((), ())),
                           preferred_element_type=_F32)


def _dot_hl(a, b):
    a_hi = a.astype(_BF16)
    a_lo = (a - a_hi.astype(_F32)).astype(_BF16)
    b_hi = b.astype(_BF16)
    b_lo = (b - b_hi.astype(_F32)).astype(_BF16)
    lhs = jnp.concatenate([a_hi, a_hi, a_lo], axis=1)
    rhs = jnp.concatenate([b_hi, b_lo, b_hi], axis=0)
    return jnp.dot(lhs, rhs, preferred_element_type=_F32)


def _split3(x):
    hi = x.astype(_BF16)
    r1 = x - hi.astype(_F32)
    mid = r1.astype(_BF16)
    lo = (r1 - mid.astype(_F32)).astype(_BF16)
    return hi, mid, lo


def _cumsum_rows(tri, x):
    hi, mid, lo = _split3(x)
    d = lambda p: jnp.dot(tri, p, preferred_element_type=_F32)
    return d(hi) + d(mid) + d(lo)


def _sigmoid(x):
    return 1.0 / (1.0 + jnp.exp(-x))


def _silu(x):
    return x * _sigmoid(x)


def _softplus(x):
    return jnp.maximum(x, 0.0) + jnp.log1p(jnp.exp(-jnp.abs(x)))


def _rms_rows(x):
    return x * lax.rsqrt(jnp.mean(x * x, axis=-1, keepdims=True) + EPS)


def _bcast_rows(ref, row, bt, ts):
    v = ref[:, row:row + 1, :]
    return jnp.broadcast_to(v, (bt, ts, v.shape[-1])).reshape(bt * ts, v.shape[-1])


def _block_row_bcast(x, block, row):
    r, n = x.shape
    x3 = x.reshape(r // block, block, n)
    return jnp.broadcast_to(x3[:, row:row + 1, :], x3.shape).reshape(r, n)


def _level_reference(g, m, rows):
    if 2 * m >= 8:
        return _block_row_bcast(g, 2 * m, m - 1)
    r = g.shape[0]
    pos = rows & (2 * m - 1)
    out = g
    for p in range(2 * m):
        shift = p - (m - 1)
        if shift == 0:
            continue
        out = jnp.where(pos == p, pltpu.roll(g, shift % r, axis=0), out)
    return out


def _chained(*gens):
    for g in gens:
        yield from g


def _alternating(a, b):
    live = [a, b]
    while live:
        for g in list(live):
            try:
                next(g)
                yield
            except StopIteration:
                live.remove(g)


def _run_interleaved(chain_steps, n_chain, fill_steps, n_fill):
    per = -(-n_fill // n_chain)
    for _ in chain_steps:
        for _ in range(per):
            next(fill_steps, None)
    for _ in fill_steps:
        pass


def _mod_kernel(c_ref, w_ref, b_ref, o_ref):
    a = _silu(c_ref[...])
    o_ref[...] = _dot(a, w_ref[...]) + b_ref[...]


def _modulation(c_all, w_ada, b_ada):
    rows = c_all.shape[0]
    tn = 1024
    n = N_MOD * D_MODEL
    return pl.pallas_call(
        _mod_kernel,
        grid=(DEPTH, n // tn),
        in_specs=[
            pl.BlockSpec((rows, D_MODEL), lambda l, j: (0, 0)),
            pl.BlockSpec((None, D_MODEL, tn), lambda l, j: (l, 0, j)),
            pl.BlockSpec((None, 1, tn), lambda l, j: (l, 0, j)),
        ],
        out_specs=pl.BlockSpec((None, rows, tn), lambda l, j: (l, 0, j)),
        out_shape=jax.ShapeDtypeStruct((DEPTH, rows, n), _F32),
        compiler_params=pltpu.CompilerParams(dimension_semantics=("arbitrary", "arbitrary")),
        name="adaln_modulation",
    )(c_all, w_ada, b_ada.reshape(DEPTH, 1, n))


def _mixer_kernel(*refs, layer, bt, ts, chunk, hchunk, carry, ffn=None):
    if ffn is not None:
        (x_ref, mod_ref, modf_ref, lb_ref, nmix_ref, w1_ref, wab_ref, w2_ref, hgn_ref, cw_ref, alog_ref,
         dtb_ref, gdn_ref, wpa_ref, wpb_ref, wout_ref, trih_ref, trig_ref, code_ref, bd_ref, bdk_ref,
         nffn_ref, wup_ref, wdn_ref, fin_ref,
         xo_ref, shg_ref, sgd_ref, cvo_ref, ext_ref, xmid_ref, act_ref) = refs
    elif carry:
        (x_ref, mod_ref, lb_ref, nmix_ref, w1_ref, wab_ref, w2_ref, hgn_ref, cw_ref, alog_ref, dtb_ref,
         gdn_ref, wpa_ref, wpb_ref, wout_ref, trih_ref, trig_ref, code_ref, bd_ref, bdk_ref,
         xo_ref, shg_ref, sgd_ref, cvo_ref, ext_ref) = refs
    else:
        (x_ref, mod_ref, shg0_ref, sgd0_ref, cv0_ref, lb_ref, nmix_ref, w1_ref, wab_ref, w2_ref, hgn_ref,
         cw_ref, alog_ref, dtb_ref, gdn_ref, wpa_ref, wpb_ref, wout_ref, trih_ref, trig_ref, code_ref, bd_ref,
         bdk_ref, xo_ref, shg_ref, sgd_ref, cvo_ref, ext_ref) = refs

    r = bt * ts
    n_seq = bt
    n_chunks = r // chunk
    chunks_per_seq = ts // chunk
    n_hchunks = r // hchunk
    hchunks_per_seq = ts // hchunk

    if carry:
        if ffn is not None:
            nt, n_tiles, last = ffn
            step = pl.program_id(0)
            live = step < n_tiles
            first = jnp.logical_and(step % nt == 0, live)

            @pl.when(step == 0)
            def _():
                xmid_ref[...] = jnp.zeros_like(xmid_ref)
        else:
            first = pl.program_id(1) == 0

        @pl.when(first)
        def _():
            shg_ref[...] = jnp.zeros_like(shg_ref)
            sgd_ref[...] = jnp.zeros_like(sgd_ref)
            ext_ref[:, PAD - (CONV_W - 1):PAD, :] = jnp.zeros((bt, CONV_W - 1, CONV_CH), _F32)
    else:
        shg_ref[...] = shg0_ref[...]
        sgd_ref[...] = sgd0_ref[...]
        ext_ref[:, PAD - (CONV_W - 1):PAD, :] = cv0_ref[...]

    x = x_ref[...].reshape(r, D_MODEL)
    sh1 = _bcast_rows(mod_ref, 0, bt, ts)
    sc1 = _bcast_rows(mod_ref, 1, bt, ts)
    h = _rms_rows(x) * nmix_ref[...] * (1.0 + sc1) + sh1
    hb = h.astype(_BF16)

    ffn_out = []

    def ffn_steps():
        xm = xmid_ref[...].reshape(r, D_MODEL)
        hf2 = (_rms_rows(xm) * nffn_ref[...] * (1.0 + _bcast_rows(modf_ref, 4, bt, ts))
               + _bcast_rows(modf_ref, 3, bt, ts)).astype(_BF16)
        yield
        for lo in range(0, D_FF, MXU_COLS):
            gate = jnp.dot(hf2, wup_ref[:, lo:lo + MXU_COLS], preferred_element_type=_F32)
            up = jnp.dot(hf2, wup_ref[:, D_FF + lo:D_FF + lo + MXU_COLS], preferred_element_type=_F32)
            act_ref[:, lo:lo + MXU_COLS] = (_silu(gate) * up).astype(_BF16)
            yield
        parts = []
        for lo in range(0, D_MODEL, MXU_COLS):
            parts.append(jnp.dot(act_ref[...], wdn_ref[:, lo:lo + MXU_COLS], preferred_element_type=_F32))
            yield
        xn = xm + _bcast_rows(modf_ref, 5, bt, ts) * jnp.concatenate(parts, axis=-1)
        if last:
            xn = _rms_rows(xn) * fin_ref[...]
        ffn_out.append(xn)
        yield

    n_ffn = 2 + D_FF // MXU_COLS + D_MODEL // MXU_COLS
    ffn_fill = ffn_steps() if ffn is not None else iter(())

    rows = lax.broadcasted_iota(jnp.int32, (r, 1), 0)
    tri_h = trih_ref[...]
    tri_g = trig_ref[...]
    code = code_ref[...]

    lbp = lb_ref[...]
    e = jnp.exp(lbp - jnp.max(lbp, axis=0, keepdims=True))
    sm = e / jnp.sum(e, axis=0, keepdims=True)
    cs0 = sm[0:1, :]
    csl = cs0
    for i in range(1, layer + 1):
        csl = csl + sm[i:i + 1, :]
    lb = csl - cs0

    def proj(w_ref, lo):
        return _dot(hb, w_ref[:, lo:lo + MXU_COLS])

    n_hp = QK // MXU_COLS
    la = jnp.log(lb)
    l1m = jnp.log1p(-lb)
    cw = cw_ref[...]
    base = PAD - (CONV_W - 1)

    def gate_math(hf_p, cs):
        e_hf = jnp.exp(-jnp.abs(hf_p))
        log_sig = jnp.minimum(hf_p, 0.0) - jnp.log(1.0 + e_hf)
        sig_neg = jnp.where(hf_p > 0.0, e_hf, 1.0) / (1.0 + e_hf)
        lbv = l1m[:, cs] + log_sig
        lav = la[:, cs]
        log_f = jnp.maximum(lav, lbv) + jnp.log(1.0 + jnp.exp(-jnp.abs(lav - lbv)))
        return log_f, (1.0 - lb[:, cs]) * sig_neg

    def decay_math(g_p, hq_p, k_p):
        g_last_p = _block_row_bcast(g_p, hchunk, hchunk - 1)
        return ((hq_p * jnp.exp(g_p)).astype(_BF16), (k_p * jnp.exp(g_last_p - g_p)).astype(_BF16),
                jnp.exp(g_last_p))

    def conv_math(j):
        cs = slice(j * MXU_COLS, (j + 1) * MXU_COLS)
        y = ext_ref[:, base:base + ts, cs] * cw[0:1, cs]
        for tap in range(1, CONV_W):
            y = y + ext_ref[:, base + tap:base + tap + ts, cs] * cw[tap:tap + 1, cs]
        tail = ext_ref[:, PAD + ts - (CONV_W - 1):PAD + ts, cs]
        cvo_ref[:, :, cs] = tail
        ext_ref[:, PAD - (CONV_W - 1):PAD, cs] = tail
        return _silu(y).reshape(r, MXU_COLS)

    def l2n_heads(z):
        parts = []
        for hd in range(z.shape[1] // DK):
            zh = z[:, hd * DK:(hd + 1) * DK]
            parts.append(zh * lax.rsqrt(jnp.sum(zh * zh, axis=-1, keepdims=True) + EPS))
        return jnp.concatenate(parts, axis=-1)

    def conv_proj(j):
        ext_ref[:, PAD:PAD + ts, j * MXU_COLS:(j + 1) * MXU_COLS] = (
            proj(w1_ref, _C_CONV + j * MXU_COLS).reshape(bt, ts, MXU_COLS))

    hf_p = [proj(w1_ref, QK + j * MXU_COLS) for j in range(n_hp)]
    gm, hq_p, hi_p, g_p, dm = [], [], [], [], []
    for j in range(n_hp):
        gm.append(gate_math(hf_p[j], slice(j * MXU_COLS, (j + 1) * MXU_COLS)))
        hq_p.append(proj(w1_ref, j * MXU_COLS) * DK ** -0.5)
    for j in range(n_hp):
        g_p.append(_cumsum_rows(tri_h, gm[j][0]))
        hi_p.append(proj(w1_ref, 2 * QK + j * MXU_COLS))
    for j in range(n_hp):
        dm.append(decay_math(g_p[j], hq_p[j], gm[j][1]))
        conv_proj(j)
    conv_p = []
    n_cp = CONV_CH // MXU_COLS
    next(ffn_fill, None)
    for j in range(n_hp, n_cp):
        conv_proj(j)
        conv_p.append(conv_math(j - n_hp))
        next(ffn_fill, None)
    gab = _dot(hb, wab_ref[...])
    for j in range(n_cp - n_hp, n_cp):
        conv_p.append(conv_math(j))
        next(ffn_fill, None)

    hq = jnp.concatenate(hq_p, axis=-1)
    hi = jnp.concatenate(hi_p, axis=-1)
    k_hg = jnp.concatenate([m[1] for m in gm], axis=-1)
    g_hg = jnp.concatenate(g_p, axis=-1)
    qg = jnp.concatenate([m[0] for m in dm], axis=-1)
    kd = jnp.concatenate([m[1] for m in dm], axis=-1)
    dec_hg = jnp.concatenate([m[2] for m in dm], axis=-1)
    hi_b = hi.astype(_BF16)

    levels = []
    g2_hg = g_hg * LOG2E

    def level_prep(m):
        ev = jnp.exp2(-jnp.abs(g2_hg - _level_reference(g2_hg, m, rows)))
        levels.append((int(math.log2(m)), (hq * ev).astype(_BF16), (k_hg * ev).astype(_BF16)))

    level_sizes = [hchunk >> (i + 1) for i in range(int(math.log2(hchunk)))]
    hq_b = hq.astype(_BF16)
    khg_b = k_hg.astype(_BF16)

    s_hg = [[shg_ref[sq, hd] for hd in range(HEADS)] for sq in range(n_seq)]
    s_gd = [[sgd_ref[sq, hd] for hd in range(HEADS)] for sq in range(n_seq)]
    o_hg = [None] * HEADS
    o_gd = [[None] * n_chunks for _ in range(HEADS)]

    def hgrn_head_steps(hd):
        ls = slice(hd * DK, (hd + 1) * DK)
        outs = []
        for c in range(n_hchunks):
            sq = c // hchunks_per_seq
            rs = slice(c * hchunk, (c + 1) * hchunk)
            a = jnp.where(code == -1, _dot_nt(hq_b[rs, ls], khg_b[rs, ls]), 0.0)
            yield
            for (sh, qe, ke) in levels:
                a = jnp.where(code == sh, _dot_nt(qe[rs, ls], ke[rs, ls]), a)
                yield
            s = s_hg[sq][hd]
            outs.append(_dot(a, hi_b[rs, ls]) + _dot(qg[rs, ls], s))
            dcol = jnp.transpose(jnp.broadcast_to(dec_hg[c * hchunk:c * hchunk + 1, ls], (DK, DK)))
            s_hg[sq][hd] = dcol * s + _dot_tn(kd[rs, ls], hi_b[rs, ls])
            yield
        o_hg[hd] = outs[0] if len(outs) == 1 else jnp.concatenate(outs, axis=0)

    log_a = -jnp.exp(alog_ref[...]) * _softplus(gab + dtb_ref[...])
    beta = _sigmoid(gab)
    g_gd = _cumsum_rows(tri_g, log_a)
    g_gd_t = jnp.transpose(g_gd)

    n_qp = QK // MXU_COLS
    qn = jnp.concatenate([l2n_heads(p) for p in conv_p[:n_qp]], axis=-1) * DK ** -0.5
    kn = jnp.concatenate([l2n_heads(p) for p in conv_p[n_qp:2 * n_qp]], axis=-1)
    vv = jnp.concatenate(conv_p[2 * n_qp:], axis=-1)
    qn_b = qn.astype(_BF16)
    kn_b = kn.astype(_BF16)

    wide = HEADS * chunk
    lsh = int(math.log2(chunk))
    w_lane = lax.broadcasted_iota(jnp.int32, (chunk, wide), 1)
    w_row = lax.broadcasted_iota(jnp.int32, (chunk, wide), 0)
    w_head = w_lane >> lsh
    w_pos = w_lane & (chunk - 1)
    incl_w = w_pos <= w_row
    strict_w = w_pos < w_row
    eye_w = jnp.where(w_pos == w_row, 1.0, 0.0)
    bd_b = bd_ref[...]
    bdk_b = bdk_ref[...]

    def block_diag(xb):
        return jnp.concatenate([xb] * HEADS, axis=0) * bd_b

    def head_cols(arr, col0, rs):
        out = jnp.broadcast_to(arr[rs, col0:col0 + 1], (chunk, wide))
        for hd in range(1, HEADS):
            out = jnp.where(w_head == hd, arr[rs, col0 + hd:col0 + hd + 1], out)
        return out

    items = []
    for c in range(n_chunks):
        rs = slice(c * chunk, (c + 1) * chunk)
        k_c = kn_b[rs]
        bdk = jnp.concatenate([k_c] * HEADS, axis=0) * bdk_b
        kq = _dot_nt(jnp.concatenate([k_c, qn_b[rs]], axis=0), bdk)
        gc_w = head_cols(g_gd, 0, rs)
        bc_w = head_cols(beta, HEADS, rs)
        gr_w = jnp.concatenate([g_gd_t[hd:hd + 1, rs] for hd in range(HEADS)], axis=1)
        rel_w = jnp.exp(jnp.where(incl_w, gc_w - gr_w, -jnp.inf))
        p_w = jnp.where(strict_w, -(bc_w * rel_w * kq[:chunk]), 0.0)
        pkb = p_w.astype(_BF16)
        items.append(dict(c=c, rs=rs, p=p_w, pkb=pkb, bd=block_diag(pkb), t=eye_w + p_w,
                          qk=jnp.where(incl_w, kq[chunk:] * rel_w, 0.0).astype(_BF16)))

    def wy_chain():
        for _ in range(lsh - 1):
            for it in items:
                it["pkb"] = _dot(it["pkb"], it["bd"]).astype(_BF16)
                it["bd"] = block_diag(it["pkb"])
            yield
            for it in items:
                it["t"] = it["t"] + _dot(it["t"], it["bd"])
            yield

    heads = []

    def solve_chain():
        for it in items:
            rs = it["rs"]
            tb = it["t"].astype(_BF16)
            a_w = eye_w - it["p"]
            for hd in range(HEADS):
                ls = slice(hd * DK, (hd + 1) * DK)
                ws_ = slice(hd * chunk, (hd + 1) * chunk)
                gc = g_gd[rs, hd:hd + 1]
                bc = beta[rs, HEADS + hd:HEADS + hd + 1]
                eg = jnp.exp(gc)
                g_end = gc[chunk - 1:chunk, :]
                k = kn[rs, ls]
                heads.append(dict(
                    c=it["c"], hd=hd, x0=tb[:, ws_], a=a_w[:, ws_],
                    rhs=jnp.concatenate([bc * vv[rs, ls], (bc * eg) * k], axis=-1),
                    qk=it["qk"][:, ws_],
                    qe=(qn[rs, ls] * eg).astype(_BF16),
                    kdec=(k * jnp.exp(g_end - gc)).astype(_BF16),
                    dec=jnp.exp(g_end)))
        for it in heads:
            it["sol0"] = _dot(it["x0"], it["rhs"])
        yield
        for it in heads:
            it["resid"] = it["rhs"] - _dot_hl(it["a"], it["sol0"])
        yield
        for it in heads:
            sol = it["sol0"] + _dot(it["x0"], it["resid"])
            it["u0"] = sol[:, :DV]
            it["wq"] = jnp.concatenate([sol[:, DV:].astype(_BF16), it["qe"]], axis=0)
        yield

    def gdn_chain():
        for j in range(chunks_per_seq):
            cs = [sq * chunks_per_seq + j for sq in range(n_seq)]
            its = [it for c in cs for it in heads[c * HEADS:(c + 1) * HEADS]]
            wss = []
            for it in its:
                sq = it["c"] // chunks_per_seq
                wss.append(_dot(it["wq"], s_gd[sq][it["hd"]]))
            yield
            for it, ws in zip(its, wss):
                sq = it["c"] // chunks_per_seq
                hd = it["hd"]
                u = (it["u0"] - ws[:chunk]).astype(_BF16)
                o_gd[hd][it["c"]] = ws[chunk:] + _dot(it["qk"], u)
                s_gd[sq][hd] = it["dec"] * s_gd[sq][hd] + _dot_tn(it["kdec"], u)
            yield

    side = {}

    def side_steps():
        for name, w_ref, lo, hi_ in (("hog", w1_ref, 3 * QK, 4 * QK), ("gz", w2_ref, 0, QK),
                                     ("gate_a", w2_ref, QK, QK + D_MODEL),
                                     ("gate_b", w2_ref, QK + D_MODEL, QK + 2 * D_MODEL)):
            for c0 in range(lo, hi_, MXU_COLS):
                side.setdefault(name, []).append(_dot(hb, w_ref[:, c0:c0 + MXU_COLS]))
                yield

    def level_steps():
        for m in level_sizes:
            level_prep(m)
            yield

    n_head = n_hchunks * (len(level_sizes) + 2)
    n_side = (2 * QK + 2 * D_MODEL) // MXU_COLS
    _run_interleaved(wy_chain(), 2 * (lsh - 1),
                     _chained(_alternating(level_steps(), side_steps()), hgrn_head_steps(0)),
                     len(level_sizes) + n_side + n_head)
    _run_interleaved(_chained(solve_chain(), gdn_chain()), 3 + 2 * chunks_per_seq,
                     _alternating(_chained(*[hgrn_head_steps(hd) for hd in range(1, HEADS)]), ffn_fill),
                     (HEADS - 1) * n_head + (n_ffn if ffn is not None else 0))
    side = {k: jnp.concatenate(v, axis=-1) for k, v in side.items()}

    def head_norm(o, wn):
        parts = []
        for hd in range(HEADS):
            parts.append(_rms_rows(o[:, hd * DV:(hd + 1) * DV]) * wn)
        return jnp.concatenate(parts, axis=-1)

    for sq in range(n_seq):
        for hd in range(HEADS):
            if ffn is not None:
                shg_ref[sq, hd] = jnp.where(live, s_hg[sq][hd], shg_ref[sq, hd])
                sgd_ref[sq, hd] = jnp.where(live, s_gd[sq][hd], sgd_ref[sq, hd])
            else:
                shg_ref[sq, hd] = s_hg[sq][hd]
                sgd_ref[sq, hd] = s_gd[sq][hd]
    o_hg_all = jnp.concatenate(o_hg, axis=-1)
    o_gd_all = jnp.concatenate([jnp.concatenate(o_gd[hd], axis=0) for hd in range(HEADS)], axis=-1)
    o_a = head_norm(o_hg_all, hgn_ref[...]) * _silu(side["hog"])
    o_b = head_norm(o_gd_all, gdn_ref[...]) * _silu(side["gz"])
    y_a = _dot(o_a, wpa_ref[...])
    y_b = _dot(o_b, wpb_ref[...])
    ym = _sigmoid(side["gate_a"]) * y_a + _sigmoid(side["gate_b"]) * y_b
    out = _dot(ym, wout_ref[...])
    g1 = _bcast_rows(mod_ref, 2, bt, ts)
    x_new = (x + g1 * out).reshape(bt, ts, D_MODEL)
    if ffn is not None:
        for _ in ffn_fill:
            pass
        xmid_ref[...] = x_new
        xo_ref[...] = ffn_out[0].reshape(bt, ts, D_MODEL)
    else:
        xo_ref[...] = x_new


def _mixer_constants(r, chunk, hchunk):
    t = np.arange(r)[:, None]
    s = np.arange(r)[None, :]

    def block_tri(size):
        return ((t // size == s // size) & (s <= t)).astype(np.float32)

    top_bit = np.floor(np.log2(np.maximum(t ^ s, 1))).astype(np.int32)
    code = np.where(s < t, top_bit, np.where(s == t, -1, -2)).astype(np.int32)[:hchunk, :hchunk]
    wide = HEADS * chunk
    wr = np.arange(wide)[:, None] // chunk
    bd = (wr == np.arange(wide)[None, :] // chunk).astype(np.float32)
    bdk = (wr == np.arange(QK)[None, :] // DK).astype(np.float32)
    return [jnp.asarray(block_tri(hchunk), _BF16), jnp.asarray(block_tri(chunk), _BF16), jnp.asarray(code),
            jnp.asarray(bd, _BF16), jnp.asarray(bdk, _BF16)]


def _const_spec(shape):
    nd = len(shape)
    return pl.BlockSpec(shape, lambda *_: (0,) * nd, pipeline_mode=pl.Buffered(1))


def _mixer(x, mod, mod_row0, states, hg_lb, lw, *, layer, bt, ts, chunk, hchunk):
    b, t, _ = x.shape
    carry = states is None
    nb = b // bt
    nt = t // ts
    if carry:
        grid = (nb, nt)
        bmap3 = lambda i, j: (i, j, 0)
        smap4 = lambda i, j: (i, 0, 0, 0)
        smap3 = lambda i, j: (i, 0, 0)
        modmap = lambda i, j: (layer, mod_row0 // bt + i, 0, 0)
        sem = ("arbitrary", "arbitrary")
    else:
        assert nt == 1 and ts == chunk
        grid = (nb,)
        bmap3 = lambda i: (i, 0, 0)
        smap4 = lambda i: (i, 0, 0, 0)
        smap3 = lambda i: (i, 0, 0)
        modmap = lambda i: (layer, mod_row0 // bt + i, 0, 0)
        sem = ("arbitrary",)

    in_specs = [pl.BlockSpec((bt, ts, D_MODEL), bmap3),
                pl.BlockSpec((None, bt, N_MOD, D_MODEL), modmap)]
    args = [x, mod]
    if not carry:
        in_specs += [pl.BlockSpec((bt, HEADS, DK, DV), smap4),
                     pl.BlockSpec((bt, HEADS, DK, DV), smap4),
                     pl.BlockSpec((bt, CONV_W - 1, CONV_CH), smap3)]
        args += list(states)
    consts = [hg_lb, lw["norm_mix"], lw["w1"], lw["wab"], lw["w2"], lw["hg_norm"], lw["conv_w"], lw["a_log"],
              lw["dt_bias"], lw["gdn_norm"], lw["w_pa"], lw["w_pb"], lw["w_out"]]
    consts += _mixer_constants(bt * ts, chunk, hchunk)
    in_specs += [_const_spec(a.shape) for a in consts]
    args += consts

    out_shape = (jax.ShapeDtypeStruct((b, t, D_MODEL), _F32),
                 jax.ShapeDtypeStruct((b, HEADS, DK, DV), _F32),
                 jax.ShapeDtypeStruct((b, HEADS, DK, DV), _F32),
                 jax.ShapeDtypeStruct((b, CONV_W - 1, CONV_CH), _F32))
    out_specs = (pl.BlockSpec((bt, ts, D_MODEL), bmap3),
                 pl.BlockSpec((bt, HEADS, DK, DV), smap4),
                 pl.BlockSpec((bt, HEADS, DK, DV), smap4),
                 pl.BlockSpec((bt, CONV_W - 1, CONV_CH), smap3))
    return pl.pallas_call(
        functools.partial(_mixer_kernel, layer=layer, bt=bt, ts=ts, chunk=chunk, hchunk=hchunk, carry=carry),
        grid=grid, in_specs=in_specs, out_specs=out_specs, out_shape=out_shape,
        scratch_shapes=[pltpu.VMEM((bt, PAD + ts, CONV_CH), _F32)],
        compiler_params=pltpu.CompilerParams(dimension_semantics=sem, vmem_limit_bytes=VMEM_LIMIT),
        name=f"mixer_l{layer}_{'prompt' if carry else 'sample'}",
    )(*args)


def _fused_layer(x, mod, lw, hg_lb, final_norm, *, layer, ts, chunk, hchunk, last):
    b, t, _ = x.shape
    nt = t // ts
    n_tiles = b * nt
    mix_tile = lambda g: jnp.minimum(g, n_tiles - 1)
    ffn_tile = lambda g: jnp.maximum(g - 1, 0)
    consts = [hg_lb, lw["norm_mix"], lw["w1"], lw["wab"], lw["w2"], lw["hg_norm"], lw["conv_w"], lw["a_log"],
              lw["dt_bias"], lw["gdn_norm"], lw["w_pa"], lw["w_pb"], lw["w_out"]]
    consts += _mixer_constants(ts, chunk, hchunk)
    consts += [lw["norm_ffn"], lw["w_up"], lw["w_down"], final_norm]
    in_specs = [pl.BlockSpec((1, ts, D_MODEL), lambda g: (mix_tile(g) // nt, mix_tile(g) % nt, 0)),
                pl.BlockSpec((None, 1, N_MOD, D_MODEL), lambda g: (layer, mix_tile(g) // nt, 0, 0)),
                pl.BlockSpec((None, 1, N_MOD, D_MODEL), lambda g: (layer, ffn_tile(g) // nt, 0, 0))]
    in_specs += [_const_spec(a.shape) for a in consts]
    smap4 = lambda g: (mix_tile(g) // nt, 0, 0, 0)
    smap3 = lambda g: (mix_tile(g) // nt, 0, 0)
    out_shape = (jax.ShapeDtypeStruct((b, t, D_MODEL), _F32),
                 jax.ShapeDtypeStruct((b, HEADS, DK, DV), _F32),
                 jax.ShapeDtypeStruct((b, HEADS, DK, DV), _F32),
                 jax.ShapeDtypeStruct((b, CONV_W - 1, CONV_CH), _F32))
    out_specs = (pl.BlockSpec((1, ts, D_MODEL), lambda g: (ffn_tile(g) // nt, ffn_tile(g) % nt, 0)),
                 pl.BlockSpec((1, HEADS, DK, DV), smap4),
                 pl.BlockSpec((1, HEADS, DK, DV), smap4),
                 pl.BlockSpec((1, CONV_W - 1, CONV_CH), smap3))
    return pl.pallas_call(
        functools.partial(_mixer_kernel, layer=layer, bt=1, ts=ts, chunk=chunk, hchunk=hchunk, carry=True,
                          ffn=(nt, n_tiles, last)),
        grid=(n_tiles + 1,), in_specs=in_specs, out_specs=out_specs, out_shape=out_shape,
        scratch_shapes=[pltpu.VMEM((1, PAD + ts, CONV_CH), _F32),
                        pltpu.VMEM((1, ts, D_MODEL), _F32),
                        pltpu.VMEM((ts, D_FF), _BF16)],
        compiler_params=pltpu.CompilerParams(dimension_semantics=("arbitrary",),
                                             vmem_limit_bytes=FUSED_VMEM_LIMIT),
        name=f"layer_l{layer}_prompt",
    )(x, mod, mod, *consts)


FF_HALF = D_FF // 2


def _ffn_kernel(x_ref, mod_ref, nffn_ref, wup_ref, wdn_ref, fin_ref, o_ref, act_ref, *, bt, ts, last):
    r = bt * ts
    x = x_ref[...].reshape(r, D_MODEL)
    sh2 = _bcast_rows(mod_ref, 3, bt, ts)
    sc2 = _bcast_rows(mod_ref, 4, bt, ts)
    g2 = _bcast_rows(mod_ref, 5, bt, ts)
    hb = (_rms_rows(x) * nffn_ref[...] * (1.0 + sc2) + sh2).astype(_BF16)
    for j in range(2):
        lo = j * FF_HALF
        gate = jnp.dot(hb, wup_ref[:, lo:lo + FF_HALF], preferred_element_type=_F32)
        up = jnp.dot(hb, wup_ref[:, D_FF + lo:D_FF + lo + FF_HALF], preferred_element_type=_F32)
        act_ref[:, lo:lo + FF_HALF] = (_silu(gate) * up).astype(_BF16)
    out = jnp.dot(act_ref[...], wdn_ref[...], preferred_element_type=_F32)
    xn = x + g2 * out
    if last:
        xn = _rms_rows(xn) * fin_ref[...]
    o_ref[...] = xn.reshape(bt, ts, D_MODEL)


def _ffn(x, mod, mod_row0, lw, final_norm, *, layer, bt, ts, last):
    b, t, _ = x.shape
    grid = (b // bt, t // ts)
    consts = [lw["norm_ffn"], lw["w_up"], lw["w_down"], final_norm]
    r = bt * ts
    return pl.pallas_call(
        functools.partial(_ffn_kernel, bt=bt, ts=ts, last=last),
        grid=grid,
        in_specs=[pl.BlockSpec((bt, ts, D_MODEL), lambda i, j: (i, j, 0)),
                  pl.BlockSpec((None, bt, N_MOD, D_MODEL), lambda i, j: (layer, mod_row0 // bt + i, 0, 0))]
                 + [_const_spec(a.shape) for a in consts],
        out_specs=pl.BlockSpec((bt, ts, D_MODEL), lambda i, j: (i, j, 0)),
        out_shape=jax.ShapeDtypeStruct((b, t, D_MODEL), _F32),
        scratch_shapes=[pltpu.VMEM((r, D_FF), _BF16)],
        compiler_params=pltpu.CompilerParams(dimension_semantics=("arbitrary", "arbitrary"),
                                             vmem_limit_bytes=VMEM_LIMIT),
        name=f"ffn_l{layer}",
    )(x, mod, *consts)


def _layer_weights(l, norm_mix, w_in, hg_norm, conv_w, gdn_a_log, gdn_dt_bias, gdn_norm, w_proj_a, w_proj_b,
                   w_out, norm_ffn, w_up, w_down):
    pad_lanes = lambda v, off: jnp.zeros((1, 128), _F32).at[0, off:off + HEADS].set(v.astype(_F32))
    wl = w_in[l]
    wab = jnp.zeros((D_MODEL, 128), _BF16).at[:, :2 * HEADS].set(wl[:, _C_AB:_C_REST].astype(_BF16))
    return {
        "norm_mix": norm_mix[l].reshape(1, D_MODEL),
        "w1": wl[:, :_C_AB].astype(_BF16),
        "wab": wab,
        "w2": wl[:, _C_REST:].astype(_BF16),
        "hg_norm": hg_norm[l].reshape(1, DV),
        "conv_w": conv_w[l],
        "a_log": pad_lanes(gdn_a_log[l], 0),
        "dt_bias": pad_lanes(gdn_dt_bias[l], 0),
        "gdn_norm": gdn_norm[l].reshape(1, DV),
        "w_pa": w_proj_a[l].astype(_BF16),
        "w_pb": w_proj_b[l].astype(_BF16),
        "w_out": w_out[l].astype(_BF16),
        "norm_ffn": norm_ffn[l].reshape(1, D_MODEL),
        "w_up": w_up[l].astype(_BF16),
        "w_down": w_down[l].astype(_BF16),
    }


def kernel(x_prompt, x_sample, state_hgrn, state_gdn, state_conv, c_prompt, c_sample, w_ada, b_ada, norm_mix,
           w_in, hg_lb, hg_norm, conv_w, gdn_a_log, gdn_dt_bias, gdn_norm, w_proj_a, w_proj_b, w_out, norm_ffn,
           w_up, w_down, final_norm):
    n_prompt = x_prompt.shape[0]
    n_sample, t_sample, _ = x_sample.shape
    c_all = jnp.concatenate([c_prompt, c_sample], axis=0)
    mod = _modulation(c_all, w_ada, b_ada).reshape(DEPTH, n_prompt + n_sample, N_MOD, D_MODEL)
    fin = final_norm.reshape(1, D_MODEL)
    hg_lb = hg_lb.astype(_F32)

    xp, xs = x_prompt, x_sample
    hg_p, gd_p, cv_p, hg_s, gd_s, cv_s = [], [], [], [], [], []
    for l in range(DEPTH):
        lw = _layer_weights(l, norm_mix, w_in, hg_norm, conv_w, gdn_a_log, gdn_dt_bias, gdn_norm, w_proj_a,
                            w_proj_b, w_out, norm_ffn, w_up, w_down)
        last = l == DEPTH - 1
        xp, a, b, c = _fused_layer(xp, mod, lw, hg_lb, fin, layer=l, ts=256, chunk=64, hchunk=256, last=last)
        hg_p.append(a); gd_p.append(b); cv_p.append(c)
        xs, a, b, c = _mixer(xs, mod, n_prompt, (state_hgrn[l], state_gdn[l], state_conv[l]), hg_lb, lw,
                             layer=l, bt=8, ts=t_sample, chunk=t_sample, hchunk=t_sample)
        hg_s.append(a); gd_s.append(b); cv_s.append(c)
        xs = _ffn(xs, mod, n_prompt, lw, fin, layer=l, bt=8, ts=t_sample, last=last)
    st = jnp.stack
    return (xp, xs, st(hg_p), st(gd_p), st(cv_p), st(hg_s), st(gd_s), st(cv_s))
```

```python
import functools
import math

import jax
import jax.numpy as jnp
import numpy as np
from jax import lax
from jax.experimental import pallas as pl
from jax.experimental.pallas import tpu as pltpu

D_MODEL = 1024
DEPTH = 2
HEADS = 4
DK = 128
DV = 128
QK = HEADS * DK
CONV_W = 4
CONV_CH = 3 * QK
D_FF = 2816
N_MOD = 6
EPS = 1e-6
LOG2E = 1.4426950408889634
PAD = 8
MXU_COLS = 256

_C_HG = 0
_C_CONV = 4 * QK
_C_AB = _C_CONV + CONV_CH
_C_REST = _C_AB + 2 * HEADS
D_IN = _C_REST + QK + 2 * D_MODEL

VMEM_LIMIT = 56 * 1024 * 1024
FUSED_VMEM_LIMIT = 62 * 1024 * 1024

_F32 = jnp.float32
_BF16 = jnp.bfloat16


def _dot(a, b):
    return jnp.dot(a.astype(_BF16), b.astype(_BF16), preferred_element_type=_F32)


def _dot_nt(a, b):
    return lax.dot_general(a.astype(_BF16), b.astype(_BF16), (((1,), (1,)), ((), ())),
                           preferred_element_type=_F32)


def _dot_tn(a, b):
    return lax.dot_general(a.astype(_BF16), b.astype(_BF16), (((0,), (0,)), ((), ())),
                           preferred_element_type=_F32)


def _dot_hl(a, b):
    a_hi = a.astype(_BF16)
    a_lo = (a - a_hi.astype(_F32)).astype(_BF16)
    b_hi = b.astype(_BF16)
    b_lo = (b - b_hi.astype(_F32)).astype(_BF16)
    lhs = jnp.concatenate([a_hi, a_hi, a_lo], axis=1)
    rhs = jnp.concatenate([b_hi, b_lo, b_hi], axis=0)
    return jnp.dot(lhs, rhs, preferred_element_type=_F32)


def _split3(x):
    hi = x.astype(_BF16)
    r1 = x - hi.astype(_F32)
    mid = r1.astype(_BF16)
    lo = (r1 - mid.astype(_F32)).astype(_BF16)
    return hi, mid, lo


def _cumsum_rows(tri, x):
    hi, mid, lo = _split3(x)
    d = lambda p: jnp.dot(tri, p, preferred_element_type=_F32)
    return d(hi) + d(mid) + d(lo)


def _sigmoid(x):
    return 1.0 / (1.0 + jnp.exp(-x))


def _silu(x):
    return x * _sigmoid(x)


def _softplus(x):
    return jnp.maximum(x, 0.0) + jnp.log1p(jnp.exp(-jnp.abs(x)))


def _rms_rows(x):
    return x * lax.rsqrt(jnp.mean(x * x, axis=-1, keepdims=True) + EPS)


def _bcast_rows(ref, row, bt, ts):
    v = ref[:, row:row + 1, :]
    return jnp.broadcast_to(v, (bt, ts, v.shape[-1])).reshape(bt * ts, v.shape[-1])


def _block_row_bcast(x, block, row):
    r, n = x.shape
    x3 = x.reshape(r // block, block, n)
    return jnp.broadcast_to(x3[:, row:row + 1, :], x3.shape).reshape(r, n)


def _level_reference(g, m, rows):
    if 2 * m >= 8:
        return _block_row_bcast(g, 2 * m, m - 1)
    r = g.shape[0]
    pos = rows & (2 * m - 1)
    out = g
    for p in range(2 * m):
        shift = p - (m - 1)
        if shift == 0:
            continue
        out = jnp.where(pos == p, pltpu.roll(g, shift % r, axis=0), out)
    return out


def _chained(*gens):
    for g in gens:
        yield from g


def _alternating(a, b):
    live = [a, b]
    while live:
        for g in list(live):
            try:
                next(g)
                yield
            except StopIteration:
                live.remove(g)


def _run_interleaved(chain_steps, n_chain, fill_steps, n_fill):
    per = -(-n_fill // n_chain)
    for _ in chain_steps:
        for _ in range(per):
            next(fill_steps, None)
    for _ in fill_steps:
        pass


def _mod_kernel(c_ref, w_ref, b_ref, o_ref):
    a = _silu(c_ref[...])
    o_ref[...] = _dot(a, w_ref[...]) + b_ref[...]


def _modulation(c_all, w_ada, b_ada):
    rows = c_all.shape[0]
    tn = 3072
    n = N_MOD * D_MODEL
    return pl.pallas_call(
        _mod_kernel,
        grid=(DEPTH, n // tn),
        in_specs=[
            pl.BlockSpec((rows, D_MODEL), lambda l, j: (0, 0)),
            pl.BlockSpec((None, D_MODEL, tn), lambda l, j: (l, 0, j)),
            pl.BlockSpec((None, 1, tn), lambda l, j: (l, 0, j)),
        ],
        out_specs=pl.BlockSpec((None, rows, tn), lambda l, j: (l, 0, j)),
        out_shape=jax.ShapeDtypeStruct((DEPTH, rows, n), _F32),
        compiler_params=pltpu.CompilerParams(dimension_semantics=("arbitrary", "arbitrary")),
        name="adaln_modulation",
    )(c_all, w_ada, b_ada.reshape(DEPTH, 1, n))


def _mixer_kernel(*refs, layer, bt, ts, chunk, hchunk, carry, ffn=None):
    if ffn is not None:
        (x_ref, mod_ref, modf_ref, lb_ref, nmix_ref, w1_ref, wab_ref, w2_ref, hgn_ref, cw_ref, alog_ref,
         dtb_ref, gdn_ref, wpa_ref, wpb_ref, wout_ref, trih_ref, trig_ref, code_ref, bd_ref, bdk_ref,
         nffn_ref, wup_ref, wdn_ref, fin_ref,
         xo_ref, shg_ref, sgd_ref, cvo_ref, ext_ref, xmid_ref, act_ref) = refs
    elif carry:
        (x_ref, mod_ref, lb_ref, nmix_ref, w1_ref, wab_ref, w2_ref, hgn_ref, cw_ref, alog_ref, dtb_ref,
         gdn_ref, wpa_ref, wpb_ref, wout_ref, trih_ref, trig_ref, code_ref, bd_ref, bdk_ref,
         xo_ref, shg_ref, sgd_ref, cvo_ref, ext_ref) = refs
    else:
        (x_ref, mod_ref, shg0_ref, sgd0_ref, cv0_ref, lb_ref, nmix_ref, w1_ref, wab_ref, w2_ref, hgn_ref,
         cw_ref, alog_ref, dtb_ref, gdn_ref, wpa_ref, wpb_ref, wout_ref, trih_ref, trig_ref, code_ref, bd_ref,
         bdk_ref, xo_ref, shg_ref, sgd_ref, cvo_ref, ext_ref) = refs

    r = bt * ts
    n_seq = bt
    n_chunks = r // chunk
    chunks_per_seq = ts // chunk
    n_hchunks = r // hchunk
    hchunks_per_seq = ts // hchunk

    if carry:
        if ffn is not None:
            nt, n_tiles, last = ffn
            step = pl.program_id(0)
            live = step < n_tiles
            first = jnp.logical_and(step % nt == 0, live)

            @pl.when(step == 0)
            def _():
                xmid_ref[...] = jnp.zeros_like(xmid_ref)
        else:
            first = pl.program_id(1) == 0

        @pl.when(first)
        def _():
            shg_ref[...] = jnp.zeros_like(shg_ref)
            sgd_ref[...] = jnp.zeros_like(sgd_ref)
            ext_ref[:, PAD - (CONV_W - 1):PAD, :] = jnp.zeros((bt, CONV_W - 1, CONV_CH), _F32)
    else:
        shg_ref[...] = shg0_ref[...]
        sgd_ref[...] = sgd0_ref[...]
        ext_ref[:, PAD - (CONV_W - 1):PAD, :] = cv0_ref[...]

    x = x_ref[...].reshape(r, D_MODEL)
    sh1 = _bcast_rows(mod_ref, 0, bt, ts)
    sc1 = _bcast_rows(mod_ref, 1, bt, ts)
    h = _rms_rows(x) * nmix_ref[...] * (1.0 + sc1) + sh1
    hb = h.astype(_BF16)

    ffn_out = []

    def ffn_steps():
        xm = xmid_ref[...].reshape(r, D_MODEL)
        hf2 = (_rms_rows(xm) * nffn_ref[...] * (1.0 + _bcast_rows(modf_ref, 4, bt, ts))
               + _bcast_rows(modf_ref, 3, bt, ts)).astype(_BF16)
        yield
        for lo in range(0, D_FF, MXU_COLS):
            gate = jnp.dot(hf2, wup_ref[:, lo:lo + MXU_COLS], preferred_element_type=_F32)
            up = jnp.dot(hf2, wup_ref[:, D_FF + lo:D_FF + lo + MXU_COLS], preferred_element_type=_F32)
            act_ref[:, lo:lo + MXU_COLS] = (_silu(gate) * up).astype(_BF16)
            yield
        parts = []
        for lo in range(0, D_MODEL, MXU_COLS):
            parts.append(jnp.dot(act_ref[...], wdn_ref[:, lo:lo + MXU_COLS], preferred_element_type=_F32))
            yield
        xn = xm + _bcast_rows(modf_ref, 5, bt, ts) * jnp.concatenate(parts, axis=-1)
        if last:
            xn = _rms_rows(xn) * fin_ref[...]
        ffn_out.append(xn)
        yield

    n_ffn = 2 + D_FF // MXU_COLS + D_MODEL // MXU_COLS
    ffn_fill = ffn_steps() if ffn is not None else iter(())

    rows = lax.broadcasted_iota(jnp.int32, (r, 1), 0)
    tri_h = trih_ref[...]
    tri_g = trig_ref[...]
    code = code_ref[...]

    lbp = lb_ref[...]
    e = jnp.exp(lbp - jnp.max(lbp, axis=0, keepdims=True))
    sm = e / jnp.sum(e, axis=0, keepdims=True)
    cs0 = sm[0:1, :]
    csl = cs0
    for i in range(1, layer + 1):
        csl = csl + sm[i:i + 1, :]
    lb = csl - cs0

    def proj(w_ref, lo):
        return _dot(hb, w_ref[:, lo:lo + MXU_COLS])

    n_hp = QK // MXU_COLS
    la = jnp.log(lb)
    l1m = jnp.log1p(-lb)
    cw = cw_ref[...]
    base = PAD - (CONV_W - 1)

    def gate_math(hf_p, cs):
        e_hf = jnp.exp(-jnp.abs(hf_p))
        log_sig = jnp.minimum(hf_p, 0.0) - jnp.log(1.0 + e_hf)
        sig_neg = jnp.where(hf_p > 0.0, e_hf, 1.0) / (1.0 + e_hf)
        lbv = l1m[:, cs] + log_sig
        lav = la[:, cs]
        log_f = jnp.maximum(lav, lbv) + jnp.log(1.0 + jnp.exp(-jnp.abs(lav - lbv)))
        return log_f, (1.0 - lb[:, cs]) * sig_neg

    def decay_math(g_p, hq_p, k_p):
        g_last_p = _block_row_bcast(g_p, hchunk, hchunk - 1)
        return ((hq_p * jnp.exp(g_p)).astype(_BF16), (k_p * jnp.exp(g_last_p - g_p)).astype(_BF16),
                jnp.exp(g_last_p))

    def conv_math(j):
        cs = slice(j * MXU_COLS, (j + 1) * MXU_COLS)
        y = ext_ref[:, base:base + ts, cs] * cw[0:1, cs]
        for tap in range(1, CONV_W):
            y = y + ext_ref[:, base + tap:base + tap + ts, cs] * cw[tap:tap + 1, cs]
        tail = ext_ref[:, PAD + ts - (CONV_W - 1):PAD + ts, cs]
        cvo_ref[:, :, cs] = tail
        ext_ref[:, PAD - (CONV_W - 1):PAD, cs] = tail
        return _silu(y).reshape(r, MXU_COLS)

    def l2n_heads(z):
        parts = []
        for hd in range(z.shape[1] // DK):
            zh = z[:, hd * DK:(hd + 1) * DK]
            parts.append(zh * lax.rsqrt(jnp.sum(zh * zh, axis=-1, keepdims=True) + EPS))
        return jnp.concatenate(parts, axis=-1)

    def conv_proj(j):
        ext_ref[:, PAD:PAD + ts, j * MXU_COLS:(j + 1) * MXU_COLS] = (
            proj(w1_ref, _C_CONV + j * MXU_COLS).reshape(bt, ts, MXU_COLS))

    hf_p = [proj(w1_ref, QK + j * MXU_COLS) for j in range(n_hp)]
    gm, hq_p, hi_p, g_p, dm = [], [], [], [], []
    for j in range(n_hp):
        gm.append(gate_math(hf_p[j], slice(j * MXU_COLS, (j + 1) * MXU_COLS)))
        hq_p.append(proj(w1_ref, j * MXU_COLS) * DK ** -0.5)
    for j in range(n_hp):
        g_p.append(_cumsum_rows(tri_h, gm[j][0]))
        hi_p.append(proj(w1_ref, 2 * QK + j * MXU_COLS))
    for j in range(n_hp):
        dm.append(decay_math(g_p[j], hq_p[j], gm[j][1]))
        conv_proj(j)
    conv_p = []
    n_cp = CONV_CH // MXU_COLS
    next(ffn_fill, None)
    for j in range(n_hp, n_cp):
        conv_proj(j)
        conv_p.append(conv_math(j - n_hp))
        if j % 2:
            next(ffn_fill, None)
    gab = _dot(hb, wab_ref[...])
    for j in range(n_cp - n_hp, n_cp):
        conv_p.append(conv_math(j))
        next(ffn_fill, None)

    hq = jnp.concatenate(hq_p, axis=-1)
    hi = jnp.concatenate(hi_p, axis=-1)
    k_hg = jnp.concatenate([m[1] for m in gm], axis=-1)
    g_hg = jnp.concatenate(g_p, axis=-1)
    qg = jnp.concatenate([m[0] for m in dm], axis=-1)
    kd = jnp.concatenate([m[1] for m in dm], axis=-1)
    dec_hg = jnp.concatenate([m[2] for m in dm], axis=-1)
    hi_b = hi.astype(_BF16)

    levels = []
    g2_hg = g_hg * LOG2E

    def level_prep(m):
        ev = jnp.exp2(-jnp.abs(g2_hg - _level_reference(g2_hg, m, rows)))
        levels.append((int(math.log2(m)), (hq * ev).astype(_BF16), (k_hg * ev).astype(_BF16)))

    level_sizes = [hchunk >> (i + 1) for i in range(int(math.log2(hchunk)))]
    hq_b = hq.astype(_BF16)
    khg_b = k_hg.astype(_BF16)

    s_hg = [[shg_ref[sq, hd] for hd in range(HEADS)] for sq in range(n_seq)]
    s_gd = [[sgd_ref[sq, hd] for hd in range(HEADS)] for sq in range(n_seq)]
    o_hg = [None] * HEADS
    o_gd = [[None] * n_chunks for _ in range(HEADS)]

    def hgrn_head_steps(hd):
        ls = slice(hd * DK, (hd + 1) * DK)
        outs = []
        for c in range(n_hchunks):
            sq = c // hchunks_per_seq
            rs = slice(c * hchunk, (c + 1) * hchunk)
            a = jnp.where(code == -1, _dot_nt(hq_b[rs, ls], khg_b[rs, ls]), 0.0)
            yield
            for (sh, qe, ke) in levels:
                a = jnp.where(code == sh, _dot_nt(qe[rs, ls], ke[rs, ls]), a)
                yield
            s = s_hg[sq][hd]
            outs.append(_dot(a, hi_b[rs, ls]) + _dot(qg[rs, ls], s))
            dcol = jnp.transpose(jnp.broadcast_to(dec_hg[c * hchunk:c * hchunk + 1, ls], (DK, DK)))
            s_hg[sq][hd] = dcol * s + _dot_tn(kd[rs, ls], hi_b[rs, ls])
            yield
        o_hg[hd] = outs[0] if len(outs) == 1 else jnp.concatenate(outs, axis=0)

    log_a = -jnp.exp(alog_ref[...]) * _softplus(gab + dtb_ref[...])
    beta = _sigmoid(gab)
    g_gd = _cumsum_rows(tri_g, log_a)
    g_gd_t = jnp.transpose(g_gd)

    n_qp = QK // MXU_COLS
    qn = jnp.concatenate([l2n_heads(p) for p in conv_p[:n_qp]], axis=-1) * DK ** -0.5
    kn = jnp.concatenate([l2n_heads(p) for p in conv_p[n_qp:2 * n_qp]], axis=-1)
    vv = jnp.concatenate(conv_p[2 * n_qp:], axis=-1)
    qn_b = qn.astype(_BF16)
    kn_b = kn.astype(_BF16)

    wide = HEADS * chunk
    lsh = int(math.log2(chunk))
    w_lane = lax.broadcasted_iota(jnp.int32, (chunk, wide), 1)
    w_row = lax.broadcasted_iota(jnp.int32, (chunk, wide), 0)
    w_head = w_lane >> lsh
    w_pos = w_lane & (chunk - 1)
    incl_w = w_pos <= w_row
    strict_w = w_pos < w_row
    eye_w = jnp.where(w_pos == w_row, 1.0, 0.0)
    bd_b = bd_ref[...]
    bdk_b = bdk_ref[...]

    def block_diag(xb):
        return jnp.concatenate([xb] * HEADS, axis=0) * bd_b

    def head_cols(arr, col0, rs):
        out = jnp.broadcast_to(arr[rs, col0:col0 + 1], (chunk, wide))
        for hd in range(1, HEADS):
            out = jnp.where(w_head == hd, arr[rs, col0 + hd:col0 + hd + 1], out)
        return out

    items = []
    for c in range(n_chunks):
        rs = slice(c * chunk, (c + 1) * chunk)
        k_c = kn_b[rs]
        bdk = jnp.concatenate([k_c] * HEADS, axis=0) * bdk_b
        kq = _dot_nt(jnp.concatenate([k_c, qn_b[rs]], axis=0), bdk)
        gc_w = head_cols(g_gd, 0, rs)
        bc_w = head_cols(beta, HEADS, rs)
        gr_w = jnp.concatenate([g_gd_t[hd:hd + 1, rs] for hd in range(HEADS)], axis=1)
        rel_w = jnp.exp(jnp.where(incl_w, gc_w - gr_w, -jnp.inf))
        p_w = jnp.where(strict_w, -(bc_w * rel_w * kq[:chunk]), 0.0)
        pkb = p_w.astype(_BF16)
        items.append(dict(c=c, rs=rs, p=p_w, pkb=pkb, bd=block_diag(pkb), t=eye_w + p_w,
                          qk=jnp.where(incl_w, kq[chunk:] * rel_w, 0.0).astype(_BF16)))
        next(ffn_fill, None)

    def wy_chain():
        for _ in range(lsh - 1):
            for it in items:
                it["pkb"] = _dot(it["pkb"], it["bd"]).astype(_BF16)
                it["bd"] = block_diag(it["pkb"])
            yield
            for it in items:
                it["t"] = it["t"] + _dot(it["t"], it["bd"])
            yield

    heads = []

    def solve_chain():
        for it in items:
            rs = it["rs"]
            tb = it["t"].astype(_BF16)
            a_w = eye_w - it["p"]
            for hd in range(HEADS):
                ls = slice(hd * DK, (hd + 1) * DK)
                ws_ = slice(hd * chunk, (hd + 1) * chunk)
                gc = g_gd[rs, hd:hd + 1]
                bc = beta[rs, HEADS + hd:HEADS + hd + 1]
                eg = jnp.exp(gc)
                g_end = gc[chunk - 1:chunk, :]
                k = kn[rs, ls]
                heads.append(dict(
                    c=it["c"], hd=hd, x0=tb[:, ws_], a=a_w[:, ws_],
                    rhs=jnp.concatenate([bc * vv[rs, ls], (bc * eg) * k], axis=-1),
                    qk=it["qk"][:, ws_],
                    qe=(qn[rs, ls] * eg).astype(_BF16),
                    kdec=(k * jnp.exp(g_end - gc)).astype(_BF16),
                    dec=jnp.exp(g_end)))
        for it in heads:
            it["sol0"] = _dot(it["x0"], it["rhs"])
        yield
        for it in heads:
            it["resid"] = it["rhs"] - _dot_hl(it["a"], it["sol0"])
        yield
        for it in heads:
            sol = it["sol0"] + _dot(it["x0"], it["resid"])
            it["u0"] = sol[:, :DV]
            it["wq"] = jnp.concatenate([sol[:, DV:].astype(_BF16), it["qe"]], axis=0)
        yield

    def gdn_chain():
        for j in range(chunks_per_seq):
            cs = [sq * chunks_per_seq + j for sq in range(n_seq)]
            its = [it for c in cs for it in heads[c * HEADS:(c + 1) * HEADS]]
            wss = []
            for it in its:
                sq = it["c"] // chunks_per_seq
                wss.append(_dot(it["wq"], s_gd[sq][it["hd"]]))
            yield
            for it, ws in zip(its, wss):
                sq = it["c"] // chunks_per_seq
                hd = it["hd"]
                u = (it["u0"] - ws[:chunk]).astype(_BF16)
                o_gd[hd][it["c"]] = ws[chunk:] + _dot(it["qk"], u)
                s_gd[sq][hd] = it["dec"] * s_gd[sq][hd] + _dot_tn(it["kdec"], u)
            yield

    side = {}

    def side_steps():
        for name, w_ref, lo, hi_ in (("hog", w1_ref, 3 * QK, 4 * QK), ("gz", w2_ref, 0, QK),
                                     ("gate_a", w2_ref, QK, QK + D_MODEL),
                                     ("gate_b", w2_ref, QK + D_MODEL, QK + 2 * D_MODEL)):
            for c0 in range(lo, hi_, MXU_COLS):
                side.setdefault(name, []).append(_dot(hb, w_ref[:, c0:c0 + MXU_COLS]))
                yield

    def level_steps():
        for m in level_sizes:
            level_prep(m)
            yield

    n_head = n_hchunks * (len(level_sizes) + 2)
    n_side = (2 * QK + 2 * D_MODEL) // MXU_COLS
    _run_interleaved(wy_chain(), 2 * (lsh - 1),
                     _chained(_alternating(level_steps(), side_steps()), hgrn_head_steps(0)),
                     len(level_sizes) + n_side + n_head)
    _run_interleaved(_chained(solve_chain(), gdn_chain()), 3 + 2 * chunks_per_seq,
                     _alternating(_chained(*[hgrn_head_steps(hd) for hd in range(1, HEADS)]), ffn_fill),
                     (HEADS - 1) * n_head + (n_ffn if ffn is not None else 0))
    side = {k: jnp.concatenate(v, axis=-1) for k, v in side.items()}

    def head_norm(o, wn):
        parts = []
        for hd in range(HEADS):
            parts.append(_rms_rows(o[:, hd * DV:(hd + 1) * DV]) * wn)
        return jnp.concatenate(parts, axis=-1)

    for sq in range(n_seq):
        for hd in range(HEADS):
            if ffn is not None:
                shg_ref[sq, hd] = jnp.where(live, s_hg[sq][hd], shg_ref[sq, hd])
                sgd_ref[sq, hd] = jnp.where(live, s_gd[sq][hd], sgd_ref[sq, hd])
            else:
                shg_ref[sq, hd] = s_hg[sq][hd]
                sgd_ref[sq, hd] = s_gd[sq][hd]
    o_hg_all = jnp.concatenate(o_hg, axis=-1)
    o_gd_all = jnp.concatenate([jnp.concatenate(o_gd[hd], axis=0) for hd in range(HEADS)], axis=-1)
    o_a = head_norm(o_hg_all, hgn_ref[...]) * _silu(side["hog"])
    o_b = head_norm(o_gd_all, gdn_ref[...]) * _silu(side["gz"])
    y_a = _dot(o_a, wpa_ref[...])
    y_b = _dot(o_b, wpb_ref[...])
    ym = _sigmoid(side["gate_a"]) * y_a + _sigmoid(side["gate_b"]) * y_b
    out = _dot(ym, wout_ref[...])
    g1 = _bcast_rows(mod_ref, 2, bt, ts)
    x_new = (x + g1 * out).reshape(bt, ts, D_MODEL)
    if ffn is not None:
        for _ in ffn_fill:
            pass
        xmid_ref[...] = x_new
        xo_ref[...] = ffn_out[0].reshape(bt, ts, D_MODEL)
    else:
        xo_ref[...] = x_new


def _mixer_constants(r, chunk, hchunk):
    t = np.arange(r)[:, None]
    s = np.arange(r)[None, :]

    def block_tri(size):
        return ((t // size == s // size) & (s <= t)).astype(np.float32)

    top_bit = np.floor(np.log2(np.maximum(t ^ s, 1))).astype(np.int32)
    code = np.where(s < t, top_bit, np.where(s == t, -1, -2)).astype(np.int32)[:hchunk, :hchunk]
    wide = HEADS * chunk
    wr = np.arange(wide)[:, None] // chunk
    bd = (wr == np.arange(wide)[None, :] // chunk).astype(np.float32)
    bdk = (wr == np.arange(QK)[None, :] // DK).astype(np.float32)
    return [jnp.asarray(block_tri(hchunk), _BF16), jnp.asarray(block_tri(chunk), _BF16), jnp.asarray(code),
            jnp.asarray(bd, _BF16), jnp.asarray(bdk, _BF16)]


def _const_spec(shape):
    nd = len(shape)
    return pl.BlockSpec(shape, lambda *_: (0,) * nd, pipeline_mode=pl.Buffered(1))


def _mixer(x, mod, mod_row0, states, hg_lb, lw, *, layer, bt, ts, chunk, hchunk):
    b, t, _ = x.shape
    carry = states is None
    nb = b // bt
    nt = t // ts
    if carry:
        grid = (nb, nt)
        bmap3 = lambda i, j: (i, j, 0)
        smap4 = lambda i, j: (i, 0, 0, 0)
        smap3 = lambda i, j: (i, 0, 0)
        modmap = lambda i, j: (layer, mod_row0 // bt + i, 0, 0)
        sem = ("arbitrary", "arbitrary")
    else:
        assert nt == 1 and ts == chunk
        grid = (nb,)
        bmap3 = lambda i: (i, 0, 0)
        smap4 = lambda i: (i, 0, 0, 0)
        smap3 = lambda i: (i, 0, 0)
        modmap = lambda i: (layer, mod_row0 // bt + i, 0, 0)
        sem = ("arbitrary",)

    in_specs = [pl.BlockSpec((bt, ts, D_MODEL), bmap3),
                pl.BlockSpec((None, bt, N_MOD, D_MODEL), modmap)]
    args = [x, mod]
    if not carry:
        in_specs += [pl.BlockSpec((None, bt, HEADS, DK, DV), lambda i: (layer, i, 0, 0, 0)),
                     pl.BlockSpec((None, bt, HEADS, DK, DV), lambda i: (layer, i, 0, 0, 0)),
                     pl.BlockSpec((None, bt, CONV_W - 1, CONV_CH), lambda i: (layer, i, 0, 0))]
        args += list(states)
    consts = [hg_lb, lw["norm_mix"], lw["w1"], lw["wab"], lw["w2"], lw["hg_norm"], lw["conv_w"], lw["a_log"],
              lw["dt_bias"], lw["gdn_norm"], lw["w_pa"], lw["w_pb"], lw["w_out"]]
    consts += _mixer_constants(bt * ts, chunk, hchunk)
    in_specs += [_const_spec(a.shape) for a in consts]
    args += consts

    out_shape = (jax.ShapeDtypeStruct((b, t, D_MODEL), _F32),
                 jax.ShapeDtypeStruct((b, HEADS, DK, DV), _F32),
                 jax.ShapeDtypeStruct((b, HEADS, DK, DV), _F32),
                 jax.ShapeDtypeStruct((b, CONV_W - 1, CONV_CH), _F32))
    out_specs = (pl.BlockSpec((bt, ts, D_MODEL), bmap3),
                 pl.BlockSpec((bt, HEADS, DK, DV), smap4),
                 pl.BlockSpec((bt, HEADS, DK, DV), smap4),
                 pl.BlockSpec((bt, CONV_W - 1, CONV_CH), smap3))
    return pl.pallas_call(
        functools.partial(_mixer_kernel, layer=layer, bt=bt, ts=ts, chunk=chunk, hchunk=hchunk, carry=carry),
        grid=grid, in_specs=in_specs, out_specs=out_specs, out_shape=out_shape,
        scratch_shapes=[pltpu.VMEM((bt, PAD + ts, CONV_CH), _F32)],
        compiler_params=pltpu.CompilerParams(dimension_semantics=sem, vmem_limit_bytes=VMEM_LIMIT),
        name=f"mixer_l{layer}_{'prompt' if carry else 'sample'}",
    )(*args)


def _fused_layer(x, mod, lw, hg_lb, final_norm, *, layer, ts, chunk, hchunk, last):
    b, t, _ = x.shape
    nt = t // ts
    n_tiles = b * nt
    mix_tile = lambda g: jnp.minimum(g, n_tiles - 1)
    ffn_tile = lambda g: jnp.maximum(g - 1, 0)
    consts = [hg_lb, lw["norm_mix"], lw["w1"], lw["wab"], lw["w2"], lw["hg_norm"], lw["conv_w"], lw["a_log"],
              lw["dt_bias"], lw["gdn_norm"], lw["w_pa"], lw["w_pb"], lw["w_out"]]
    consts += _mixer_constants(ts, chunk, hchunk)
    consts += [lw["norm_ffn"], lw["w_up"], lw["w_down"], final_norm]
    in_specs = [pl.BlockSpec((1, ts, D_MODEL), lambda g: (mix_tile(g) // nt, mix_tile(g) % nt, 0)),
                pl.BlockSpec((None, 1, N_MOD, D_MODEL), lambda g: (layer, mix_tile(g) // nt, 0, 0)),
                pl.BlockSpec((None, 1, N_MOD, D_MODEL), lambda g: (layer, ffn_tile(g) // nt, 0, 0))]
    in_specs += [_const_spec(a.shape) for a in consts]
    smap4 = lambda g: (mix_tile(g) // nt, 0, 0, 0)
    smap3 = lambda g: (mix_tile(g) // nt, 0, 0)
    out_shape = (jax.ShapeDtypeStruct((b, t, D_MODEL), _F32),
                 jax.ShapeDtypeStruct((b, HEADS, DK, DV), _F32),
                 jax.ShapeDtypeStruct((b, HEADS, DK, DV), _F32),
                 jax.ShapeDtypeStruct((b, CONV_W - 1, CONV_CH), _F32))
    out_specs = (pl.BlockSpec((1, ts, D_MODEL), lambda g: (ffn_tile(g) // nt, ffn_tile(g) % nt, 0)),
                 pl.BlockSpec((1, HEADS, DK, DV), smap4),
                 pl.BlockSpec((1, HEADS, DK, DV), smap4),
                 pl.BlockSpec((1, CONV_W - 1, CONV_CH), smap3))
    return pl.pallas_call(
        functools.partial(_mixer_kernel, layer=layer, bt=1, ts=ts, chunk=chunk, hchunk=hchunk, carry=True,
                          ffn=(nt, n_tiles, last)),
        grid=(n_tiles + 1,), in_specs=in_specs, out_specs=out_specs, out_shape=out_shape,
        scratch_shapes=[pltpu.VMEM((1, PAD + ts, CONV_CH), _F32),
                        pltpu.VMEM((1, ts, D_MODEL), _F32),
                        pltpu.VMEM((ts, D_FF), _BF16)],
        compiler_params=pltpu.CompilerParams(dimension_semantics=("arbitrary",),
                                             vmem_limit_bytes=FUSED_VMEM_LIMIT),
        name=f"layer_l{layer}_prompt",
    )(x, mod, mod, *consts)


FF_HALF = D_FF // 2


def _ffn_kernel(x_ref, mod_ref, nffn_ref, wup_ref, wdn_ref, fin_ref, o_ref, act_ref, *, bt, ts, last):
    r = bt * ts
    x = x_ref[...].reshape(r, D_MODEL)
    sh2 = _bcast_rows(mod_ref, 3, bt, ts)
    sc2 = _bcast_rows(mod_ref, 4, bt, ts)
    g2 = _bcast_rows(mod_ref, 5, bt, ts)
    hb = (_rms_rows(x) * nffn_ref[...] * (1.0 + sc2) + sh2).astype(_BF16)
    for j in range(2):
        lo = j * FF_HALF
        gate = jnp.dot(hb, wup_ref[:, lo:lo + FF_HALF], preferred_element_type=_F32)
        up = jnp.dot(hb, wup_ref[:, D_FF + lo:D_FF + lo + FF_HALF], preferred_element_type=_F32)
        act_ref[:, lo:lo + FF_HALF] = (_silu(gate) * up).astype(_BF16)
    out = jnp.dot(act_ref[...], wdn_ref[...], preferred_element_type=_F32)
    xn = x + g2 * out
    if last:
        xn = _rms_rows(xn) * fin_ref[...]
    o_ref[...] = xn.reshape(bt, ts, D_MODEL)


def _ffn(x, mod, mod_row0, lw, final_norm, *, layer, bt, ts, last):
    b, t, _ = x.shape
    grid = (b // bt, t // ts)
    consts = [lw["norm_ffn"], lw["w_up"], lw["w_down"], final_norm]
    r = bt * ts
    return pl.pallas_call(
        functools.partial(_ffn_kernel, bt=bt, ts=ts, last=last),
        grid=grid,
        in_specs=[pl.BlockSpec((bt, ts, D_MODEL), lambda i, j: (i, j, 0)),
                  pl.BlockSpec((None, bt, N_MOD, D_MODEL), lambda i, j: (layer, mod_row0 // bt + i, 0, 0))]
                 + [_const_spec(a.shape) for a in consts],
        out_specs=pl.BlockSpec((bt, ts, D_MODEL), lambda i, j: (i, j, 0)),
        out_shape=jax.ShapeDtypeStruct((b, t, D_MODEL), _F32),
        scratch_shapes=[pltpu.VMEM((r, D_FF), _BF16)],
        compiler_params=pltpu.CompilerParams(dimension_semantics=("arbitrary", "arbitrary"),
                                             vmem_limit_bytes=VMEM_LIMIT),
        name=f"ffn_l{layer}",
    )(x, mod, *consts)


def _layer_weights(l, norm_mix, w_in, hg_norm, conv_w, gdn_a_log, gdn_dt_bias, gdn_norm, w_proj_a, w_proj_b,
                   w_out, norm_ffn, w_up, w_down):
    pad_lanes = lambda v, off: jnp.zeros((1, 128), _F32).at[0, off:off + HEADS].set(v.astype(_F32))
    wl = w_in[l]
    wab = jnp.zeros((D_MODEL, 128), _BF16).at[:, :2 * HEADS].set(wl[:, _C_AB:_C_REST].astype(_BF16))
    return {
        "norm_mix": norm_mix[l].reshape(1, D_MODEL),
        "w1": wl[:, :_C_AB].astype(_BF16),
        "wab": wab,
        "w2": wl[:, _C_REST:].astype(_BF16),
        "hg_norm": hg_norm[l].reshape(1, DV),
        "conv_w": conv_w[l],
        "a_log": pad_lanes(gdn_a_log[l], 0),
        "dt_bias": pad_lanes(gdn_dt_bias[l], 0),
        "gdn_norm": gdn_norm[l].reshape(1, DV),
        "w_pa": w_proj_a[l].astype(_BF16),
        "w_pb": w_proj_b[l].astype(_BF16),
        "w_out": w_out[l].astype(_BF16),
        "norm_ffn": norm_ffn[l].reshape(1, D_MODEL),
        "w_up": w_up[l].astype(_BF16),
        "w_down": w_down[l].astype(_BF16),
    }


def kernel(x_prompt, x_sample, state_hgrn, state_gdn, state_conv, c_prompt, c_sample, w_ada, b_ada, norm_mix,
           w_in, hg_lb, hg_norm, conv_w, gdn_a_log, gdn_dt_bias, gdn_norm, w_proj_a, w_proj_b, w_out, norm_ffn,
           w_up, w_down, final_norm):
    n_prompt = x_prompt.shape[0]
    n_sample, t_sample, _ = x_sample.shape
    c_all = jnp.concatenate([c_prompt, c_sample], axis=0)
    mod = _modulation(c_all, w_ada, b_ada).reshape(DEPTH, n_prompt + n_sample, N_MOD, D_MODEL)
    fin = final_norm.reshape(1, D_MODEL)
    hg_lb = hg_lb.astype(_F32)

    xp, xs = x_prompt, x_sample
    hg_p, gd_p, cv_p, hg_s, gd_s, cv_s = [], [], [], [], [], []
    for l in range(DEPTH):
        lw = _layer_weights(l, norm_mix, w_in, hg_norm, conv_w, gdn_a_log, gdn_dt_bias, gdn_norm, w_proj_a,
                            w_proj_b, w_out, norm_ffn, w_up, w_down)
        last = l == DEPTH - 1
        xp, a, b, c = _fused_layer(xp, mod, lw, hg_lb, fin, layer=l, ts=256, chunk=64, hchunk=128, last=last)
        hg_p.append(a); gd_p.append(b); cv_p.append(c)
        xs, a, b, c = _mixer(xs, mod, n_prompt, (state_hgrn, state_gdn, state_conv), hg_lb, lw,
                             layer=l, bt=8, ts=t_sample, chunk=t_sample, hchunk=t_sample)
        hg_s.append(a); gd_s.append(b); cv_s.append(c)
        xs = _ffn(xs, mod, n_prompt, lw, fin, layer=l, bt=8, ts=t_sample, last=last)
    st = jnp.stack
    return (xp, xs, st(hg_p), st(gd_p), st(cv_p), st(hg_s), st(gd_s), st(cv_s))
```

```python
import functools
import math

import jax
import jax.numpy as jnp
import numpy as np
from jax import lax
from jax.experimental import pallas as pl
from jax.experimental.pallas import tpu as pltpu

D_MODEL = 1024
DEPTH = 2
HEADS = 4
DK = 128
DV = 128
QK = HEADS * DK
CONV_W = 4
CONV_CH = 3 * QK
D_FF = 2816
N_MOD = 6
EPS = 1e-6
LOG2E = 1.4426950408889634
PAD = 8
MXU_COLS = 256

_C_HG = 0
_C_CONV = 4 * QK
_C_AB = _C_CONV + CONV_CH
_C_REST = _C_AB + 2 * HEADS
D_IN = _C_REST + QK + 2 * D_MODEL

VMEM_LIMIT = 56 * 1024 * 1024
FUSED_VMEM_LIMIT = 62 * 1024 * 1024

_F32 = jnp.float32
_BF16 = jnp.bfloat16


def _dot(a, b):
    return jnp.dot(a.astype(_BF16), b.astype(_BF16), preferred_element_type=_F32)


def _dot_nt(a, b):
    return lax.dot_general(a.astype(_BF16), b.astype(_BF16), (((1,), (1,)), ((), ())),
                           preferred_element_type=_F32)


def _dot_tn(a, b):
    return lax.dot_general(a.astype(_BF16), b.astype(_BF16), (((0,), (0,)), ((), ())),
                           preferred_element_type=_F32)


def _dot_hl(a, b):
    a_hi = a.astype(_BF16)
    a_lo = (a - a_hi.astype(_F32)).astype(_BF16)
    b_hi = b.astype(_BF16)
    b_lo = (b - b_hi.astype(_F32)).astype(_BF16)
    lhs = jnp.concatenate([a_hi, a_hi, a_lo], axis=1)
    rhs = jnp.concatenate([b_hi, b_lo, b_hi], axis=0)
    return jnp.dot(lhs, rhs, preferred_element_type=_F32)


def _split3(x):
    hi = x.astype(_BF16)
    r1 = x - hi.astype(_F32)
    mid = r1.astype(_BF16)
    lo = (r1 - mid.astype(_F32)).astype(_BF16)
    return hi, mid, lo


def _cumsum_rows(tri, x):
    hi, mid, lo = _split3(x)
    d = lambda p: jnp.dot(tri, p, preferred_element_type=_F32)
    return d(hi) + d(mid) + d(lo)


def _sigmoid(x):
    return 1.0 / (1.0 + jnp.exp(-x))


def _silu(x):
    return x * _sigmoid(x)


def _softplus(x):
    return jnp.maximum(x, 0.0) + jnp.log1p(jnp.exp(-jnp.abs(x)))


def _rms_rows(x):
    return x * lax.rsqrt(jnp.mean(x * x, axis=-1, keepdims=True) + EPS)


def _bcast_rows(ref, row, bt, ts):
    v = ref[:, row:row + 1, :]
    return jnp.broadcast_to(v, (bt, ts, v.shape[-1])).reshape(bt * ts, v.shape[-1])


def _block_row_bcast(x, block, row):
    r, n = x.shape
    x3 = x.reshape(r // block, block, n)
    return jnp.broadcast_to(x3[:, row:row + 1, :], x3.shape).reshape(r, n)


def _level_reference(g, m, rows):
    if 2 * m >= 8:
        return _block_row_bcast(g, 2 * m, m - 1)
    r = g.shape[0]
    pos = rows & (2 * m - 1)
    out = g
    for p in range(2 * m):
        shift = p - (m - 1)
        if shift == 0:
            continue
        out = jnp.where(pos == p, pltpu.roll(g, shift % r, axis=0), out)
    return out


def _chained(*gens):
    for g in gens:
        yield from g


def _alternating(a, b):
    live = [a, b]
    while live:
        for g in list(live):
            try:
                next(g)
                yield
            except StopIteration:
                live.remove(g)


def _run_interleaved(chain_steps, n_chain, fill_steps, n_fill):
    per = -(-n_fill // n_chain)
    for _ in chain_steps:
        for _ in range(per):
            next(fill_steps, None)
    for _ in fill_steps:
        pass


def _mod_kernel(c_ref, w_ref, b_ref, o_ref):
    a = _silu(c_ref[...])
    o_ref[...] = _dot(a, w_ref[...]) + b_ref[...]


def _modulation(c_all, w_ada, b_ada):
    rows = c_all.shape[0]
    tn = 3072
    n = N_MOD * D_MODEL
    return pl.pallas_call(
        _mod_kernel,
        grid=(DEPTH, n // tn),
        in_specs=[
            pl.BlockSpec((rows, D_MODEL), lambda l, j: (0, 0)),
            pl.BlockSpec((None, D_MODEL, tn), lambda l, j: (l, 0, j)),
            pl.BlockSpec((None, 1, tn), lambda l, j: (l, 0, j)),
        ],
        out_specs=pl.BlockSpec((None, rows, tn), lambda l, j: (l, 0, j)),
        out_shape=jax.ShapeDtypeStruct((DEPTH, rows, n), _F32),
        compiler_params=pltpu.CompilerParams(dimension_semantics=("arbitrary", "arbitrary")),
        name="adaln_modulation",
    )(c_all, w_ada, b_ada.reshape(DEPTH, 1, n))


def _mixer_kernel(*refs, layer, bt, ts, chunk, hchunk, carry, ffn=None):
    if ffn is not None:
        (x_ref, mod_ref, modf_ref, lb_ref, nmix_ref, w1_ref, wab_ref, w2_ref, hgn_ref, cw_ref, alog_ref,
         dtb_ref, gdn_ref, wpa_ref, wpb_ref, wout_ref, trih_ref, trig_ref, code_ref, bd_ref, bdk_ref, codew_ref,
         nffn_ref, wup_ref, wdn_ref, fin_ref,
         xo_ref, shg_ref, sgd_ref, cvo_ref, ext_ref, xmid_ref, act_ref) = refs
    elif carry:
        (x_ref, mod_ref, lb_ref, nmix_ref, w1_ref, wab_ref, w2_ref, hgn_ref, cw_ref, alog_ref, dtb_ref,
         gdn_ref, wpa_ref, wpb_ref, wout_ref, trih_ref, trig_ref, code_ref, bd_ref, bdk_ref, codew_ref,
         xo_ref, shg_ref, sgd_ref, cvo_ref, ext_ref) = refs
    else:
        (x_ref, mod_ref, shg0_ref, sgd0_ref, cv0_ref, lb_ref, nmix_ref, w1_ref, wab_ref, w2_ref, hgn_ref,
         cw_ref, alog_ref, dtb_ref, gdn_ref, wpa_ref, wpb_ref, wout_ref, trih_ref, trig_ref, code_ref, bd_ref,
         bdk_ref, codew_ref, xo_ref, shg_ref, sgd_ref, cvo_ref, ext_ref) = refs

    r = bt * ts
    n_seq = bt
    n_chunks = r // chunk
    chunks_per_seq = ts // chunk
    n_hchunks = r // hchunk
    hchunks_per_seq = ts // hchunk

    if carry:
        if ffn is not None:
            nt, n_tiles, last = ffn
            step = pl.program_id(0)
            live = step < n_tiles
            first = jnp.logical_and(step % nt == 0, live)

            @pl.when(step == 0)
            def _():
                xmid_ref[...] = jnp.zeros_like(xmid_ref)
        else:
            first = pl.program_id(1) == 0

        @pl.when(first)
        def _():
            shg_ref[...] = jnp.zeros_like(shg_ref)
            sgd_ref[...] = jnp.zeros_like(sgd_ref)
            ext_ref[:, PAD - (CONV_W - 1):PAD, :] = jnp.zeros((bt, CONV_W - 1, CONV_CH), _F32)
    else:
        shg_ref[...] = shg0_ref[...]
        sgd_ref[...] = sgd0_ref[...]
        ext_ref[:, PAD - (CONV_W - 1):PAD, :] = cv0_ref[...]

    x = x_ref[...].reshape(r, D_MODEL)
    sh1 = _bcast_rows(mod_ref, 0, bt, ts)
    sc1 = _bcast_rows(mod_ref, 1, bt, ts)
    h = _rms_rows(x) * nmix_ref[...] * (1.0 + sc1) + sh1
    hb = h.astype(_BF16)

    ffn_out = []

    def ffn_steps():
        xm = xmid_ref[...].reshape(r, D_MODEL)
        hf2 = (_rms_rows(xm) * nffn_ref[...] * (1.0 + _bcast_rows(modf_ref, 4, bt, ts))
               + _bcast_rows(modf_ref, 3, bt, ts)).astype(_BF16)
        yield
        for lo in range(0, D_FF, MXU_COLS):
            gate = jnp.dot(hf2, wup_ref[:, lo:lo + MXU_COLS], preferred_element_type=_F32)
            up = jnp.dot(hf2, wup_ref[:, D_FF + lo:D_FF + lo + MXU_COLS], preferred_element_type=_F32)
            act_ref[:, lo:lo + MXU_COLS] = (_silu(gate) * up).astype(_BF16)
            yield
        parts = []
        for lo in range(0, D_MODEL, MXU_COLS):
            parts.append(jnp.dot(act_ref[...], wdn_ref[:, lo:lo + MXU_COLS], preferred_element_type=_F32))
            yield
        xn = xm + _bcast_rows(modf_ref, 5, bt, ts) * jnp.concatenate(parts, axis=-1)
        if last:
            xn = _rms_rows(xn) * fin_ref[...]
        ffn_out.append(xn)
        yield

    n_ffn = 2 + D_FF // MXU_COLS + D_MODEL // MXU_COLS
    ffn_fill = ffn_steps() if ffn is not None else iter(())

    rows = lax.broadcasted_iota(jnp.int32, (r, 1), 0)
    tri_h = trih_ref[...]
    tri_g = trig_ref[...]
    code = code_ref[...]

    lbp = lb_ref[...]
    e = jnp.exp(lbp - jnp.max(lbp, axis=0, keepdims=True))
    sm = e / jnp.sum(e, axis=0, keepdims=True)
    cs0 = sm[0:1, :]
    csl = cs0
    for i in range(1, layer + 1):
        csl = csl + sm[i:i + 1, :]
    lb = csl - cs0

    def proj(w_ref, lo):
        return _dot(hb, w_ref[:, lo:lo + MXU_COLS])

    n_hp = QK // MXU_COLS
    la = jnp.log(lb)
    l1m = jnp.log1p(-lb)
    cw = cw_ref[...]
    base = PAD - (CONV_W - 1)

    def gate_math(hf_p, cs):
        e_hf = jnp.exp(-jnp.abs(hf_p))
        log_sig = jnp.minimum(hf_p, 0.0) - jnp.log(1.0 + e_hf)
        sig_neg = jnp.where(hf_p > 0.0, e_hf, 1.0) / (1.0 + e_hf)
        lbv = l1m[:, cs] + log_sig
        lav = la[:, cs]
        log_f = jnp.maximum(lav, lbv) + jnp.log(1.0 + jnp.exp(-jnp.abs(lav - lbv)))
        return log_f, (1.0 - lb[:, cs]) * sig_neg

    def decay_math(g_p, hq_p, k_p):
        g_last_p = _block_row_bcast(g_p, hchunk, hchunk - 1)
        return ((hq_p * jnp.exp(g_p)).astype(_BF16), (k_p * jnp.exp(g_last_p - g_p)).astype(_BF16),
                jnp.exp(g_last_p))

    def conv_math(j):
        cs = slice(j * MXU_COLS, (j + 1) * MXU_COLS)
        y = ext_ref[:, base:base + ts, cs] * cw[0:1, cs]
        for tap in range(1, CONV_W):
            y = y + ext_ref[:, base + tap:base + tap + ts, cs] * cw[tap:tap + 1, cs]
        tail = ext_ref[:, PAD + ts - (CONV_W - 1):PAD + ts, cs]
        cvo_ref[:, :, cs] = tail
        ext_ref[:, PAD - (CONV_W - 1):PAD, cs] = tail
        return _silu(y).reshape(r, MXU_COLS)

    def l2n_heads(z):
        parts = []
        for hd in range(z.shape[1] // DK):
            zh = z[:, hd * DK:(hd + 1) * DK]
            parts.append(zh * lax.rsqrt(jnp.sum(zh * zh, axis=-1, keepdims=True) + EPS))
        return jnp.concatenate(parts, axis=-1)

    def conv_proj(j):
        ext_ref[:, PAD:PAD + ts, j * MXU_COLS:(j + 1) * MXU_COLS] = (
            proj(w1_ref, _C_CONV + j * MXU_COLS).reshape(bt, ts, MXU_COLS))

    hf_p = [proj(w1_ref, QK + j * MXU_COLS) for j in range(n_hp)]
    gm, hq_p, hi_p, g_p, dm = [], [], [], [], []
    for j in range(n_hp):
        gm.append(gate_math(hf_p[j], slice(j * MXU_COLS, (j + 1) * MXU_COLS)))
        hq_p.append(proj(w1_ref, j * MXU_COLS) * DK ** -0.5)
    for j in range(n_hp):
        g_p.append(_cumsum_rows(tri_h, gm[j][0]))
        hi_p.append(proj(w1_ref, 2 * QK + j * MXU_COLS))
    for j in range(n_hp):
        dm.append(decay_math(g_p[j], hq_p[j], gm[j][1]))
        conv_proj(j)
    conv_p = []
    n_cp = CONV_CH // MXU_COLS
    next(ffn_fill, None)
    for j in range(n_hp, n_cp):
        conv_proj(j)
        conv_p.append(conv_math(j - n_hp))
        if j % 2:
            next(ffn_fill, None)
    gab = _dot(hb, wab_ref[...])
    for j in range(n_cp - n_hp, n_cp):
        conv_p.append(conv_math(j))
        next(ffn_fill, None)

    hq = jnp.concatenate(hq_p, axis=-1)
    hi = jnp.concatenate(hi_p, axis=-1)
    k_hg = jnp.concatenate([m[1] for m in gm], axis=-1)
    g_hg = jnp.concatenate(g_p, axis=-1)
    qg = jnp.concatenate([m[0] for m in dm], axis=-1)
    kd = jnp.concatenate([m[1] for m in dm], axis=-1)
    dec_hg = jnp.concatenate([m[2] for m in dm], axis=-1)
    hi_b = hi.astype(_BF16)

    levels = []
    g2_hg = g_hg * LOG2E

    def level_prep(m):
        ev = jnp.exp2(-jnp.abs(g2_hg - _level_reference(g2_hg, m, rows)))
        levels.append((int(math.log2(m)), (hq * ev).astype(_BF16), (k_hg * ev).astype(_BF16)))

    level_sizes = [hchunk >> (i + 1) for i in range(int(math.log2(hchunk)))]
    hq_b = hq.astype(_BF16)
    khg_b = k_hg.astype(_BF16)

    s_hg = [[shg_ref[sq, hd] for hd in range(HEADS)] for sq in range(n_seq)]
    s_gd = [[sgd_ref[sq, hd] for hd in range(HEADS)] for sq in range(n_seq)]
    o_hg = [None] * HEADS
    o_gd = [[None] * n_chunks for _ in range(HEADS)]

    def hgrn_head_steps(hd):
        ls = slice(hd * DK, (hd + 1) * DK)
        outs = []
        for c in range(n_hchunks):
            sq = c // hchunks_per_seq
            rs = slice(c * hchunk, (c + 1) * hchunk)
            a = jnp.where(code == -1, _dot_nt(hq_b[rs, ls], khg_b[rs, ls]), 0.0)
            yield
            for (sh, qe, ke) in levels:
                a = jnp.where(code == sh, _dot_nt(qe[rs, ls], ke[rs, ls]), a)
                yield
            s = s_hg[sq][hd]
            outs.append(_dot(a, hi_b[rs, ls]) + _dot(qg[rs, ls], s))
            dcol = jnp.transpose(jnp.broadcast_to(dec_hg[c * hchunk:c * hchunk + 1, ls], (DK, DK)))
            s_hg[sq][hd] = dcol * s + _dot_tn(kd[rs, ls], hi_b[rs, ls])
            yield
        o_hg[hd] = outs[0] if len(outs) == 1 else jnp.concatenate(outs, axis=0)

    log_a = -jnp.exp(alog_ref[...]) * _softplus(gab + dtb_ref[...])
    beta = _sigmoid(gab)
    g_gd = _cumsum_rows(tri_g, log_a)
    g_gd_t = jnp.transpose(g_gd)

    n_qp = QK // MXU_COLS
    qn = jnp.concatenate([l2n_heads(p) for p in conv_p[:n_qp]], axis=-1) * DK ** -0.5
    kn = jnp.concatenate([l2n_heads(p) for p in conv_p[n_qp:2 * n_qp]], axis=-1)
    vv = jnp.concatenate(conv_p[2 * n_qp:], axis=-1)
    qn_b = qn.astype(_BF16)
    kn_b = kn.astype(_BF16)

    wide = HEADS * chunk
    lsh = int(math.log2(chunk))
    w_lane = lax.broadcasted_iota(jnp.int32, (chunk, wide), 1)
    w_row = lax.broadcasted_iota(jnp.int32, (chunk, wide), 0)
    w_head = w_lane >> lsh
    w_pos = w_lane & (chunk - 1)
    incl_w = w_pos <= w_row
    strict_w = w_pos < w_row
    eye_w = jnp.where(w_pos == w_row, 1.0, 0.0)
    bd_b = bd_ref[...]
    bdk_b = bdk_ref[...]
    code_w = codew_ref[...]

    def block_diag(xb):
        return jnp.concatenate([xb] * HEADS, axis=0) * bd_b

    def head_cols(arr, col0, rs):
        out = jnp.broadcast_to(arr[rs, col0:col0 + 1], (chunk, wide))
        for hd in range(1, HEADS):
            out = jnp.where(w_head == hd, arr[rs, col0 + hd:col0 + hd + 1], out)
        return out

    items = []
    for c in range(n_chunks):
        rs = slice(c * chunk, (c + 1) * chunk)
        k_c = kn_b[rs]
        bdk = jnp.concatenate([k_c] * HEADS, axis=0) * bdk_b
        kq = _dot_nt(jnp.concatenate([k_c, qn_b[rs]], axis=0), bdk)
        gc_w = head_cols(g_gd, 0, rs)
        bc_w = head_cols(beta, HEADS, rs)
        gr_w = jnp.concatenate([g_gd_t[hd:hd + 1, rs] for hd in range(HEADS)], axis=1)
        rel_w = jnp.exp(jnp.where(incl_w, gc_w - gr_w, -jnp.inf))
        p_w = jnp.where(strict_w, -(bc_w * rel_w * kq[:chunk]), 0.0)
        items.append(dict(c=c, rs=rs, p=p_w, t=eye_w + jnp.where(code_w == 0, p_w, 0.0),
                          qk=jnp.where(incl_w, kq[chunk:] * rel_w, 0.0).astype(_BF16)))
        next(ffn_fill, None)

    def wy_chain():
        for sh in range(1, lsh):
            for it in items:
                it["xb"] = it["t"].astype(_BF16)
                minus_l = jnp.where(code_w == sh, it["p"], 0.0)
                it["z"] = _dot(minus_l, block_diag(it["xb"])).astype(_BF16)
            yield
            for it in items:
                it["t"] = it["t"] + _dot(it["xb"], block_diag(it["z"]))
            yield

    heads = []

    def solve_chain():
        for it in items:
            rs = it["rs"]
            tb = it["t"].astype(_BF16)
            a_w = eye_w - it["p"]
            for hd in range(HEADS):
                ls = slice(hd * DK, (hd + 1) * DK)
                ws_ = slice(hd * chunk, (hd + 1) * chunk)
                gc = g_gd[rs, hd:hd + 1]
                bc = beta[rs, HEADS + hd:HEADS + hd + 1]
                eg = jnp.exp(gc)
                g_end = gc[chunk - 1:chunk, :]
                k = kn[rs, ls]
                heads.append(dict(
                    c=it["c"], hd=hd, x0=tb[:, ws_], a=a_w[:, ws_],
                    rhs=jnp.concatenate([bc * vv[rs, ls], (bc * eg) * k], axis=-1),
                    qk=it["qk"][:, ws_],
                    qe=(qn[rs, ls] * eg).astype(_BF16),
                    kdec=(k * jnp.exp(g_end - gc)).astype(_BF16),
                    dec=jnp.exp(g_end)))
        for it in heads:
            it["sol0"] = _dot(it["x0"], it["rhs"])
        yield
        for it in heads:
            it["resid"] = it["rhs"] - _dot_hl(it["a"], it["sol0"])
        yield
        for it in heads:
            sol = it["sol0"] + _dot(it["x0"], it["resid"])
            it["u0"] = sol[:, :DV]
            it["wq"] = jnp.concatenate([sol[:, DV:].astype(_BF16), it["qe"]], axis=0)
        yield

    def gdn_chain():
        for j in range(chunks_per_seq):
            cs = [sq * chunks_per_seq + j for sq in range(n_seq)]
            its = [it for c in cs for it in heads[c * HEADS:(c + 1) * HEADS]]
            wss = []
            for it in its:
                sq = it["c"] // chunks_per_seq
                wss.append(_dot(it["wq"], s_gd[sq][it["hd"]]))
            yield
            for it, ws in zip(its, wss):
                sq = it["c"] // chunks_per_seq
                hd = it["hd"]
                u = (it["u0"] - ws[:chunk]).astype(_BF16)
                o_gd[hd][it["c"]] = ws[chunk:] + _dot(it["qk"], u)
                s_gd[sq][hd] = it["dec"] * s_gd[sq][hd] + _dot_tn(it["kdec"], u)
            yield

    side = {}

    def side_steps():
        for name, w_ref, lo, hi_ in (("hog", w1_ref, 3 * QK, 4 * QK), ("gz", w2_ref, 0, QK),
                                     ("gate_a", w2_ref, QK, QK + D_MODEL),
                                     ("gate_b", w2_ref, QK + D_MODEL, QK + 2 * D_MODEL)):
            for c0 in range(lo, hi_, MXU_COLS):
                side.setdefault(name, []).append(_dot(hb, w_ref[:, c0:c0 + MXU_COLS]))
                yield

    def level_steps():
        for m in level_sizes:
            level_prep(m)
            yield

    n_head = n_hchunks * (len(level_sizes) + 2)
    n_side = (2 * QK + 2 * D_MODEL) // MXU_COLS
    _run_interleaved(wy_chain(), 2 * (lsh - 1),
                     _chained(_alternating(level_steps(), side_steps()), hgrn_head_steps(0)),
                     len(level_sizes) + n_side + n_head)
    _run_interleaved(_chained(solve_chain(), gdn_chain()), 3 + 2 * chunks_per_seq,
                     _alternating(_chained(*[hgrn_head_steps(hd) for hd in range(1, HEADS)]), ffn_fill),
                     (HEADS - 1) * n_head + (n_ffn if ffn is not None else 0))
    side = {k: jnp.concatenate(v, axis=-1) for k, v in side.items()}

    def head_norm(o, wn):
        parts = []
        for hd in range(HEADS):
            parts.append(_rms_rows(o[:, hd * DV:(hd + 1) * DV]) * wn)
        return jnp.concatenate(parts, axis=-1)

    for sq in range(n_seq):
        for hd in range(HEADS):
            if ffn is not None:
                shg_ref[sq, hd] = jnp.where(live, s_hg[sq][hd], shg_ref[sq, hd])
                sgd_ref[sq, hd] = jnp.where(live, s_gd[sq][hd], sgd_ref[sq, hd])
            else:
                shg_ref[sq, hd] = s_hg[sq][hd]
                sgd_ref[sq, hd] = s_gd[sq][hd]
    o_hg_all = jnp.concatenate(o_hg, axis=-1)
    o_gd_all = jnp.concatenate([jnp.concatenate(o_gd[hd], axis=0) for hd in range(HEADS)], axis=-1)
    o_a = head_norm(o_hg_all, hgn_ref[...]) * _silu(side["hog"])
    o_b = head_norm(o_gd_all, gdn_ref[...]) * _silu(side["gz"])
    y_a = _dot(o_a, wpa_ref[...])
    y_b = _dot(o_b, wpb_ref[...])
    ym = _sigmoid(side["gate_a"]) * y_a + _sigmoid(side["gate_b"]) * y_b
    out = _dot(ym, wout_ref[...])
    g1 = _bcast_rows(mod_ref, 2, bt, ts)
    x_new = (x + g1 * out).reshape(bt, ts, D_MODEL)
    if ffn is not None:
        for _ in ffn_fill:
            pass
        xmid_ref[...] = x_new
        xo_ref[...] = ffn_out[0].reshape(bt, ts, D_MODEL)
    else:
        xo_ref[...] = x_new


def _mixer_constants(r, chunk, hchunk):
    t = np.arange(r)[:, None]
    s = np.arange(r)[None, :]

    def block_tri(size):
        return ((t // size == s // size) & (s <= t)).astype(np.float32)

    top_bit = np.floor(np.log2(np.maximum(t ^ s, 1))).astype(np.int32)
    code = np.where(s < t, top_bit, np.where(s == t, -1, -2)).astype(np.int32)
    wide = HEADS * chunk
    wr = np.arange(wide)[:, None] // chunk
    bd = (wr == np.arange(wide)[None, :] // chunk).astype(np.float32)
    bdk = (wr == np.arange(QK)[None, :] // DK).astype(np.float32)
    code_wide = np.tile(code[:chunk, :chunk], (1, HEADS))
    return [jnp.asarray(block_tri(hchunk), _BF16), jnp.asarray(block_tri(chunk), _BF16),
            jnp.asarray(code[:hchunk, :hchunk]), jnp.asarray(bd, _BF16), jnp.asarray(bdk, _BF16),
            jnp.asarray(code_wide)]


def _const_spec(shape):
    nd = len(shape)
    return pl.BlockSpec(shape, lambda *_: (0,) * nd, pipeline_mode=pl.Buffered(1))


def _mixer(x, mod, mod_row0, states, hg_lb, lw, *, layer, bt, ts, chunk, hchunk):
    b, t, _ = x.shape
    carry = states is None
    nb = b // bt
    nt = t // ts
    if carry:
        grid = (nb, nt)
        bmap3 = lambda i, j: (i, j, 0)
        smap4 = lambda i, j: (i, 0, 0, 0)
        smap3 = lambda i, j: (i, 0, 0)
        modmap = lambda i, j: (layer, mod_row0 // bt + i, 0, 0)
        sem = ("arbitrary", "arbitrary")
    else:
        assert nt == 1 and ts == chunk
        grid = (nb,)
        bmap3 = lambda i: (i, 0, 0)
        smap4 = lambda i: (i, 0, 0, 0)
        smap3 = lambda i: (i, 0, 0)
        modmap = lambda i: (layer, mod_row0 // bt + i, 0, 0)
        sem = ("arbitrary",)

    in_specs = [pl.BlockSpec((bt, ts, D_MODEL), bmap3),
                pl.BlockSpec((None, bt, N_MOD, D_MODEL), modmap)]
    args = [x, mod]
    if not carry:
        in_specs += [pl.BlockSpec((None, bt, HEADS, DK, DV), lambda i: (layer, i, 0, 0, 0)),
                     pl.BlockSpec((None, bt, HEADS, DK, DV), lambda i: (layer, i, 0, 0, 0)),
                     pl.BlockSpec((None, bt, CONV_W - 1, CONV_CH), lambda i: (layer, i, 0, 0))]
        args += list(states)
    consts = [hg_lb, lw["norm_mix"], lw["w1"], lw["wab"], lw["w2"], lw["hg_norm"], lw["conv_w"], lw["a_log"],
              lw["dt_bias"], lw["gdn_norm"], lw["w_pa"], lw["w_pb"], lw["w_out"]]
    consts += _mixer_constants(bt * ts, chunk, hchunk)
    in_specs += [_const_spec(a.shape) for a in consts]
    args += consts

    out_shape = (jax.ShapeDtypeStruct((b, t, D_MODEL), _F32),
                 jax.ShapeDtypeStruct((b, HEADS, DK, DV), _F32),
                 jax.ShapeDtypeStruct((b, HEADS, DK, DV), _F32),
                 jax.ShapeDtypeStruct((b, CONV_W - 1, CONV_CH), _F32))
    out_specs = (pl.BlockSpec((bt, ts, D_MODEL), bmap3),
                 pl.BlockSpec((bt, HEADS, DK, DV), smap4),
                 pl.BlockSpec((bt, HEADS, DK, DV), smap4),
                 pl.BlockSpec((bt, CONV_W - 1, CONV_CH), smap3))
    return pl.pallas_call(
        functools.partial(_mixer_kernel, layer=layer, bt=bt, ts=ts, chunk=chunk, hchunk=hchunk, carry=carry),
        grid=grid, in_specs=in_specs, out_specs=out_specs, out_shape=out_shape,
        scratch_shapes=[pltpu.VMEM((bt, PAD + ts, CONV_CH), _F32)],
        compiler_params=pltpu.CompilerParams(dimension_semantics=sem, vmem_limit_bytes=VMEM_LIMIT),
        name=f"mixer_l{layer}_{'prompt' if carry else 'sample'}",
    )(*args)


def _fused_layer(x, mod, lw, hg_lb, final_norm, *, layer, ts, chunk, hchunk, last):
    b, t, _ = x.shape
    nt = t // ts
    n_tiles = b * nt
    mix_tile = lambda g: jnp.minimum(g, n_tiles - 1)
    ffn_tile = lambda g: jnp.maximum(g - 1, 0)
    consts = [hg_lb, lw["norm_mix"], lw["w1"], lw["wab"], lw["w2"], lw["hg_norm"], lw["conv_w"], lw["a_log"],
              lw["dt_bias"], lw["gdn_norm"], lw["w_pa"], lw["w_pb"], lw["w_out"]]
    consts += _mixer_constants(ts, chunk, hchunk)
    consts += [lw["norm_ffn"], lw["w_up"], lw["w_down"], final_norm]
    in_specs = [pl.BlockSpec((1, ts, D_MODEL), lambda g: (mix_tile(g) // nt, mix_tile(g) % nt, 0)),
                pl.BlockSpec((None, 1, N_MOD, D_MODEL), lambda g: (layer, mix_tile(g) // nt, 0, 0)),
                pl.BlockSpec((None, 1, N_MOD, D_MODEL), lambda g: (layer, ffn_tile(g) // nt, 0, 0))]
    in_specs += [_const_spec(a.shape) for a in consts]
    smap4 = lambda g: (mix_tile(g) // nt, 0, 0, 0)
    smap3 = lambda g: (mix_tile(g) // nt, 0, 0)
    out_shape = (jax.ShapeDtypeStruct((b, t, D_MODEL), _F32),
                 jax.ShapeDtypeStruct((b, HEADS, DK, DV), _F32),
                 jax.ShapeDtypeStruct((b, HEADS, DK, DV), _F32),
                 jax.ShapeDtypeStruct((b, CONV_W - 1, CONV_CH), _F32))
    out_specs = (pl.BlockSpec((1, ts, D_MODEL), lambda g: (ffn_tile(g) // nt, ffn_tile(g) % nt, 0)),
                 pl.BlockSpec((1, HEADS, DK, DV), smap4),
                 pl.BlockSpec((1, HEADS, DK, DV), smap4),
                 pl.BlockSpec((1, CONV_W - 1, CONV_CH), smap3))
    return pl.pallas_call(
        functools.partial(_mixer_kernel, layer=layer, bt=1, ts=ts, chunk=chunk, hchunk=hchunk, carry=True,
                          ffn=(nt, n_tiles, last)),
        grid=(n_tiles + 1,), in_specs=in_specs, out_specs=out_specs, out_shape=out_shape,
        scratch_shapes=[pltpu.VMEM((1, PAD + ts, CONV_CH), _F32),
                        pltpu.VMEM((1, ts, D_MODEL), _F32),
                        pltpu.VMEM((ts, D_FF), _BF16)],
        compiler_params=pltpu.CompilerParams(dimension_semantics=("arbitrary",),
                                             vmem_limit_bytes=FUSED_VMEM_LIMIT),
        name=f"layer_l{layer}_prompt",
    )(x, mod, mod, *consts)


FF_HALF = D_FF // 2


def _ffn_kernel(x_ref, mod_ref, nffn_ref, wup_ref, wdn_ref, fin_ref, o_ref, act_ref, *, bt, ts, last):
    r = bt * ts
    x = x_ref[...].reshape(r, D_MODEL)
    sh2 = _bcast_rows(mod_ref, 3, bt, ts)
    sc2 = _bcast_rows(mod_ref, 4, bt, ts)
    g2 = _bcast_rows(mod_ref, 5, bt, ts)
    hb = (_rms_rows(x) * nffn_ref[...] * (1.0 + sc2) + sh2).astype(_BF16)
    for j in range(2):
        lo = j * FF_HALF
        gate = jnp.dot(hb, wup_ref[:, lo:lo + FF_HALF], preferred_element_type=_F32)
        up = jnp.dot(hb, wup_ref[:, D_FF + lo:D_FF + lo + FF_HALF], preferred_element_type=_F32)
        act_ref[:, lo:lo + FF_HALF] = (_silu(gate) * up).astype(_BF16)
    out = jnp.dot(act_ref[...], wdn_ref[...], preferred_element_type=_F32)
    xn = x + g2 * out
    if last:
        xn = _rms_rows(xn) * fin_ref[...]
    o_ref[...] = xn.reshape(bt, ts, D_MODEL)


def _ffn(x, mod, mod_row0, lw, final_norm, *, layer, bt, ts, last):
    b, t, _ = x.shape
    grid = (b // bt, t // ts)
    consts = [lw["norm_ffn"], lw["w_up"], lw["w_down"], final_norm]
    r = bt * ts
    return pl.pallas_call(
        functools.partial(_ffn_kernel, bt=bt, ts=ts, last=last),
        grid=grid,
        in_specs=[pl.BlockSpec((bt, ts, D_MODEL), lambda i, j: (i, j, 0)),
                  pl.BlockSpec((None, bt, N_MOD, D_MODEL), lambda i, j: (layer, mod_row0 // bt + i, 0, 0))]
                 + [_const_spec(a.shape) for a in consts],
        out_specs=pl.BlockSpec((bt, ts, D_MODEL), lambda i, j: (i, j, 0)),
        out_shape=jax.ShapeDtypeStruct((b, t, D_MODEL), _F32),
        scratch_shapes=[pltpu.VMEM((r, D_FF), _BF16)],
        compiler_params=pltpu.CompilerParams(dimension_semantics=("arbitrary", "arbitrary"),
                                             vmem_limit_bytes=VMEM_LIMIT),
        name=f"ffn_l{layer}",
    )(x, mod, *consts)


def _layer_weights(l, norm_mix, w_in, hg_norm, conv_w, gdn_a_log, gdn_dt_bias, gdn_norm, w_proj_a, w_proj_b,
                   w_out, norm_ffn, w_up, w_down):
    pad_lanes = lambda v, off: jnp.zeros((1, 128), _F32).at[0, off:off + HEADS].set(v.astype(_F32))
    wl = w_in[l]
    wab = jnp.zeros((D_MODEL, 128), _BF16).at[:, :2 * HEADS].set(wl[:, _C_AB:_C_REST].astype(_BF16))
    return {
        "norm_mix": norm_mix[l].reshape(1, D_MODEL),
        "w1": wl[:, :_C_AB].astype(_BF16),
        "wab": wab,
        "w2": wl[:, _C_REST:].astype(_BF16),
        "hg_norm": hg_norm[l].reshape(1, DV),
        "conv_w": conv_w[l],
        "a_log": pad_lanes(gdn_a_log[l], 0),
        "dt_bias": pad_lanes(gdn_dt_bias[l], 0),
        "gdn_norm": gdn_norm[l].reshape(1, DV),
        "w_pa": w_proj_a[l].astype(_BF16),
        "w_pb": w_proj_b[l].astype(_BF16),
        "w_out": w_out[l].astype(_BF16),
        "norm_ffn": norm_ffn[l].reshape(1, D_MODEL),
        "w_up": w_up[l].astype(_BF16),
        "w_down": w_down[l].astype(_BF16),
    }


def kernel(x_prompt, x_sample, state_hgrn, state_gdn, state_conv, c_prompt, c_sample, w_ada, b_ada, norm_mix,
           w_in, hg_lb, hg_norm, conv_w, gdn_a_log, gdn_dt_bias, gdn_norm, w_proj_a, w_proj_b, w_out, norm_ffn,
           w_up, w_down, final_norm):
    n_prompt = x_prompt.shape[0]
    n_sample, t_sample, _ = x_sample.shape
    c_all = jnp.concatenate([c_prompt, c_sample], axis=0)
    mod = _modulation(c_all, w_ada, b_ada).reshape(DEPTH, n_prompt + n_sample, N_MOD, D_MODEL)
    fin = final_norm.reshape(1, D_MODEL)
    hg_lb = hg_lb.astype(_F32)

    xp, xs = x_prompt, x_sample
    hg_p, gd_p, cv_p, hg_s, gd_s, cv_s = [], [], [], [], [], []
    for l in range(DEPTH):
        lw = _layer_weights(l, norm_mix, w_in, hg_norm, conv_w, gdn_a_log, gdn_dt_bias, gdn_norm, w_proj_a,
                            w_proj_b, w_out, norm_ffn, w_up, w_down)
        last = l == DEPTH - 1
        xp, a, b, c = _fused_layer(xp, mod, lw, hg_lb, fin, layer=l, ts=256, chunk=64, hchunk=128, last=last)
        hg_p.append(a); gd_p.append(b); cv_p.append(c)
        xs, a, b, c = _mixer(xs, mod, n_prompt, (state_hgrn, state_gdn, state_conv), hg_lb, lw,
                             layer=l, bt=8, ts=t_sample, chunk=t_sample, hchunk=t_sample)
        hg_s.append(a); gd_s.append(b); cv_s.append(c)
        xs = _ffn(xs, mod, n_prompt, lw, fin, layer=l, bt=8, ts=t_sample, last=last)
    st = jnp.stack
    return (xp, xs, st(hg_p), st(gd_p), st(cv_p), st(hg_s), st(gd_s), st(cv_s))
```

```python
import functools
import math

import jax
import jax.numpy as jnp
import numpy as np
from jax import lax
from jax.experimental import pallas as pl
from jax.experimental.pallas import tpu as pltpu

D_MODEL = 1024
DEPTH = 2
HEADS = 4
DK = 128
DV = 128
QK = HEADS * DK
CONV_W = 4
CONV_CH = 3 * QK
D_FF = 2816
N_MOD = 6
EPS = 1e-6
LOG2E = 1.4426950408889634
PAD = 8
MXU_COLS = 256

_C_HG = 0
_C_CONV = 4 * QK
_C_AB = _C_CONV + CONV_CH
_C_REST = _C_AB + 2 * HEADS
D_IN = _C_REST + QK + 2 * D_MODEL

VMEM_LIMIT = 56 * 1024 * 1024
FUSED_VMEM_LIMIT = 62 * 1024 * 1024

_F32 = jnp.float32
_BF16 = jnp.bfloat16


def _dot(a, b):
    return jnp.dot(a.astype(_BF16), b.astype(_BF16), preferred_element_type=_F32)


def _dot_nt(a, b):
    return lax.dot_general(a.astype(_BF16), b.astype(_BF16), (((1,), (1,)), ((), ())),
                           preferred_element_type=_F32)


def _dot_tn(a, b):
    return lax.dot_general(a.astype(_BF16), b.astype(_BF16), (((0,), (0,)), ((), ())),
                           preferred_element_type=_F32)


def _dot_hl(a, b):
    a_hi = a.astype(_BF16)
    a_lo = (a - a_hi.astype(_F32)).astype(_BF16)
    b_hi = b.astype(_BF16)
    b_lo = (b - b_hi.astype(_F32)).astype(_BF16)
    lhs = jnp.concatenate([a_hi, a_hi, a_lo], axis=1)
    rhs = jnp.concatenate([b_hi, b_lo, b_hi], axis=0)
    return jnp.dot(lhs, rhs, preferred_element_type=_F32)


def _split3(x):
    hi = x.astype(_BF16)
    r1 = x - hi.astype(_F32)
    mid = r1.astype(_BF16)
    lo = (r1 - mid.astype(_F32)).astype(_BF16)
    return hi, mid, lo


def _cumsum_rows(tri, x):
    hi, mid, lo = _split3(x)
    d = lambda p: jnp.dot(tri, p, preferred_element_type=_F32)
    return d(hi) + d(mid) + d(lo)


def _sigmoid(x):
    return 1.0 / (1.0 + jnp.exp(-x))


def _silu(x):
    return x * _sigmoid(x)


def _softplus(x):
    return jnp.maximum(x, 0.0) + jnp.log1p(jnp.exp(-jnp.abs(x)))


def _rms_rows(x):
    return x * lax.rsqrt(jnp.mean(x * x, axis=-1, keepdims=True) + EPS)


def _bcast_rows(ref, row, bt, ts):
    v = ref[:, row:row + 1, :]
    return jnp.broadcast_to(v, (bt, ts, v.shape[-1])).reshape(bt * ts, v.shape[-1])


def _block_row_bcast(x, block, row):
    r, n = x.shape
    x3 = x.reshape(r // block, block, n)
    return jnp.broadcast_to(x3[:, row:row + 1, :], x3.shape).reshape(r, n)


def _level_reference(g, m, rows):
    if 2 * m >= 8:
        return _block_row_bcast(g, 2 * m, m - 1)
    r = g.shape[0]
    pos = rows & (2 * m - 1)
    out = g
    for p in range(2 * m):
        shift = p - (m - 1)
        if shift == 0:
            continue
        out = jnp.where(pos == p, pltpu.roll(g, shift % r, axis=0), out)
    return out


def _chained(*gens):
    for g in gens:
        yield from g


def _alternating(a, b):
    live = [a, b]
    while live:
        for g in list(live):
            try:
                next(g)
                yield
            except StopIteration:
                live.remove(g)


def _run_interleaved(chain_steps, n_chain, fill_steps, n_fill):
    per = -(-n_fill // n_chain)
    for _ in chain_steps:
        for _ in range(per):
            next(fill_steps, None)
    for _ in fill_steps:
        pass


def _mod_kernel(c_ref, w_ref, b_ref, o_ref):
    a = _silu(c_ref[...])
    o_ref[...] = _dot(a, w_ref[...]) + b_ref[...]


def _modulation(c_all, w_ada, b_ada):
    rows = c_all.shape[0]
    tn = 3072
    n = N_MOD * D_MODEL
    return pl.pallas_call(
        _mod_kernel,
        grid=(DEPTH, n // tn),
        in_specs=[
            pl.BlockSpec((rows, D_MODEL), lambda l, j: (0, 0)),
            pl.BlockSpec((None, D_MODEL, tn), lambda l, j: (l, 0, j)),
            pl.BlockSpec((None, 1, tn), lambda l, j: (l, 0, j)),
        ],
        out_specs=pl.BlockSpec((None, rows, tn), lambda l, j: (l, 0, j)),
        out_shape=jax.ShapeDtypeStruct((DEPTH, rows, n), _F32),
        compiler_params=pltpu.CompilerParams(dimension_semantics=("arbitrary", "arbitrary")),
        name="adaln_modulation",
    )(c_all, w_ada, b_ada.reshape(DEPTH, 1, n))


def _mixer_kernel(*refs, layer, bt, ts, chunk, hchunk, carry, ffn=None):
    if ffn is not None:
        (x_ref, mod_ref, modf_ref, lb_ref, nmix_ref, w1_ref, wab_ref, w2_ref, hgn_ref, cw_ref, alog_ref,
         dtb_ref, gdn_ref, wpa_ref, wpb_ref, wout_ref, trih_ref, trig_ref, code_ref, bd_ref, bdk_ref, codew_ref,
         nffn_ref, wup_ref, wdn_ref, fin_ref,
         xo_ref, shg_ref, sgd_ref, cvo_ref, ext_ref, xmid_ref, act_ref) = refs
    elif carry:
        (x_ref, mod_ref, lb_ref, nmix_ref, w1_ref, wab_ref, w2_ref, hgn_ref, cw_ref, alog_ref, dtb_ref,
         gdn_ref, wpa_ref, wpb_ref, wout_ref, trih_ref, trig_ref, code_ref, bd_ref, bdk_ref, codew_ref,
         xo_ref, shg_ref, sgd_ref, cvo_ref, ext_ref) = refs
    else:
        (x_ref, mod_ref, shg0_ref, sgd0_ref, cv0_ref, lb_ref, nmix_ref, w1_ref, wab_ref, w2_ref, hgn_ref,
         cw_ref, alog_ref, dtb_ref, gdn_ref, wpa_ref, wpb_ref, wout_ref, trih_ref, trig_ref, code_ref, bd_ref,
         bdk_ref, codew_ref, xo_ref, shg_ref, sgd_ref, cvo_ref, ext_ref) = refs

    r = bt * ts
    n_seq = bt
    n_chunks = r // chunk
    chunks_per_seq = ts // chunk
    n_hchunks = r // hchunk
    hchunks_per_seq = ts // hchunk

    if carry:
        if ffn is not None:
            nt, n_tiles, last = ffn
            step = pl.program_id(0)
            live = step < n_tiles
            first = jnp.logical_and(step % nt == 0, live)

            @pl.when(step == 0)
            def _():
                xmid_ref[...] = jnp.zeros_like(xmid_ref)
        else:
            first = pl.program_id(1) == 0

        @pl.when(first)
        def _():
            shg_ref[...] = jnp.zeros_like(shg_ref)
            sgd_ref[...] = jnp.zeros_like(sgd_ref)
            ext_ref[:, PAD - (CONV_W - 1):PAD, :] = jnp.zeros((bt, CONV_W - 1, CONV_CH), _F32)
    else:
        shg_ref[...] = shg0_ref[...]
        sgd_ref[...] = sgd0_ref[...]
        ext_ref[:, PAD - (CONV_W - 1):PAD, :] = cv0_ref[...]

    x = x_ref[...].reshape(r, D_MODEL)
    sh1 = _bcast_rows(mod_ref, 0, bt, ts)
    sc1 = _bcast_rows(mod_ref, 1, bt, ts)
    h = _rms_rows(x) * nmix_ref[...] * (1.0 + sc1) + sh1
    hb = h.astype(_BF16)

    ffn_out = []

    def ffn_steps():
        xm = xmid_ref[...].reshape(r, D_MODEL)
        hf2 = (_rms_rows(xm) * nffn_ref[...] * (1.0 + _bcast_rows(modf_ref, 4, bt, ts))
               + _bcast_rows(modf_ref, 3, bt, ts)).astype(_BF16)
        yield
        for lo in range(0, D_FF, MXU_COLS):
            gate = jnp.dot(hf2, wup_ref[:, lo:lo + MXU_COLS], preferred_element_type=_F32)
            up = jnp.dot(hf2, wup_ref[:, D_FF + lo:D_FF + lo + MXU_COLS], preferred_element_type=_F32)
            act_ref[:, lo:lo + MXU_COLS] = (_silu(gate) * up).astype(_BF16)
            yield
        parts = []
        for lo in range(0, D_MODEL, MXU_COLS):
            parts.append(jnp.dot(act_ref[...], wdn_ref[:, lo:lo + MXU_COLS], preferred_element_type=_F32))
            yield
        xn = xm + _bcast_rows(modf_ref, 5, bt, ts) * jnp.concatenate(parts, axis=-1)
        if last:
            xn = _rms_rows(xn) * fin_ref[...]
        ffn_out.append(xn)
        yield

    n_ffn = 2 + D_FF // MXU_COLS + D_MODEL // MXU_COLS
    ffn_fill = ffn_steps() if ffn is not None else iter(())

    rows = lax.broadcasted_iota(jnp.int32, (r, 1), 0)
    tri_h = trih_ref[...]
    tri_g = trig_ref[...]
    code = code_ref[...]

    lbp = lb_ref[...]
    e = jnp.exp(lbp - jnp.max(lbp, axis=0, keepdims=True))
    sm = e / jnp.sum(e, axis=0, keepdims=True)
    cs0 = sm[0:1, :]
    csl = cs0
    for i in range(1, layer + 1):
        csl = csl + sm[i:i + 1, :]
    lb = csl - cs0

    def proj(w_ref, lo):
        return _dot(hb, w_ref[:, lo:lo + MXU_COLS])

    n_hp = QK // MXU_COLS
    la = jnp.log(lb)
    l1m = jnp.log1p(-lb)
    cw = cw_ref[...]
    base = PAD - (CONV_W - 1)

    def gate_math(hf_p, cs):
        e_hf = jnp.exp(-jnp.abs(hf_p))
        log_sig = jnp.minimum(hf_p, 0.0) - jnp.log(1.0 + e_hf)
        sig_neg = jnp.where(hf_p > 0.0, e_hf, 1.0) / (1.0 + e_hf)
        lbv = l1m[:, cs] + log_sig
        lav = la[:, cs]
        log_f = jnp.maximum(lav, lbv) + jnp.log(1.0 + jnp.exp(-jnp.abs(lav - lbv)))
        return log_f, (1.0 - lb[:, cs]) * sig_neg

    def decay_math(g_p, hq_p, k_p):
        g_last_p = _block_row_bcast(g_p, hchunk, hchunk - 1)
        return ((hq_p * jnp.exp(g_p)).astype(_BF16), (k_p * jnp.exp(g_last_p - g_p)).astype(_BF16),
                jnp.exp(g_last_p))

    def conv_math(j):
        cs = slice(j * MXU_COLS, (j + 1) * MXU_COLS)
        y = ext_ref[:, base:base + ts, cs] * cw[0:1, cs]
        for tap in range(1, CONV_W):
            y = y + ext_ref[:, base + tap:base + tap + ts, cs] * cw[tap:tap + 1, cs]
        tail = ext_ref[:, PAD + ts - (CONV_W - 1):PAD + ts, cs]
        cvo_ref[:, :, cs] = tail
        ext_ref[:, PAD - (CONV_W - 1):PAD, cs] = tail
        return _silu(y).reshape(r, MXU_COLS)

    def l2n_heads(z):
        parts = []
        for hd in range(z.shape[1] // DK):
            zh = z[:, hd * DK:(hd + 1) * DK]
            parts.append(zh * lax.rsqrt(jnp.sum(zh * zh, axis=-1, keepdims=True) + EPS))
        return jnp.concatenate(parts, axis=-1)

    def conv_proj(j):
        ext_ref[:, PAD:PAD + ts, j * MXU_COLS:(j + 1) * MXU_COLS] = (
            proj(w1_ref, _C_CONV + j * MXU_COLS).reshape(bt, ts, MXU_COLS))

    hf_p = [proj(w1_ref, QK + j * MXU_COLS) for j in range(n_hp)]
    gm, hq_p, hi_p, g_p, dm = [], [], [], [], []
    for j in range(n_hp):
        gm.append(gate_math(hf_p[j], slice(j * MXU_COLS, (j + 1) * MXU_COLS)))
        hq_p.append(proj(w1_ref, j * MXU_COLS) * DK ** -0.5)
    for j in range(n_hp):
        g_p.append(_cumsum_rows(tri_h, gm[j][0]))
        hi_p.append(proj(w1_ref, 2 * QK + j * MXU_COLS))
    for j in range(n_hp):
        dm.append(decay_math(g_p[j], hq_p[j], gm[j][1]))
        conv_proj(j)
    conv_p = []
    n_cp = CONV_CH // MXU_COLS
    next(ffn_fill, None)
    for j in range(n_hp, n_cp):
        conv_proj(j)
        conv_p.append(conv_math(j - n_hp))
        if j % 2:
            next(ffn_fill, None)
    gab = _dot(hb, wab_ref[...])
    for j in range(n_cp - n_hp, n_cp):
        conv_p.append(conv_math(j))
        next(ffn_fill, None)

    hq = jnp.concatenate(hq_p, axis=-1)
    hi = jnp.concatenate(hi_p, axis=-1)
    k_hg = jnp.concatenate([m[1] for m in gm], axis=-1)
    g_hg = jnp.concatenate(g_p, axis=-1)
    qg = jnp.concatenate([m[0] for m in dm], axis=-1)
    kd = jnp.concatenate([m[1] for m in dm], axis=-1)
    dec_hg = jnp.concatenate([m[2] for m in dm], axis=-1)
    hi_b = hi.astype(_BF16)

    levels = []
    g2_hg = g_hg * LOG2E

    def level_prep(m):
        ev = jnp.exp2(-jnp.abs(g2_hg - _level_reference(g2_hg, m, rows)))
        levels.append((int(math.log2(m)), (hq * ev).astype(_BF16), (k_hg * ev).astype(_BF16)))

    level_sizes = [hchunk >> (i + 1) for i in range(int(math.log2(hchunk)))]
    hq_b = hq.astype(_BF16)
    khg_b = k_hg.astype(_BF16)

    s_hg = [[shg_ref[sq, hd] for hd in range(HEADS)] for sq in range(n_seq)]
    s_gd = [[sgd_ref[sq, hd] for hd in range(HEADS)] for sq in range(n_seq)]
    o_hg = [None] * HEADS
    o_gd = [[None] * n_chunks for _ in range(HEADS)]

    def hgrn_head_steps(hd):
        ls = slice(hd * DK, (hd + 1) * DK)
        outs = []
        for c in range(n_hchunks):
            sq = c // hchunks_per_seq
            rs = slice(c * hchunk, (c + 1) * hchunk)
            a = jnp.where(code == -1, _dot_nt(hq_b[rs, ls], khg_b[rs, ls]), 0.0)
            yield
            for (sh, qe, ke) in levels:
                a = jnp.where(code == sh, _dot_nt(qe[rs, ls], ke[rs, ls]), a)
                yield
            s = s_hg[sq][hd]
            outs.append(_dot(a, hi_b[rs, ls]) + _dot(qg[rs, ls], s))
            dcol = jnp.transpose(jnp.broadcast_to(dec_hg[c * hchunk:c * hchunk + 1, ls], (DK, DK)))
            s_hg[sq][hd] = dcol * s + _dot_tn(kd[rs, ls], hi_b[rs, ls])
            yield
        o_hg[hd] = outs[0] if len(outs) == 1 else jnp.concatenate(outs, axis=0)

    log_a = -jnp.exp(alog_ref[...]) * _softplus(gab + dtb_ref[...])
    beta = _sigmoid(gab)
    g_gd = _cumsum_rows(tri_g, log_a)
    g_gd_t = jnp.transpose(g_gd)

    n_qp = QK // MXU_COLS
    qn = jnp.concatenate([l2n_heads(p) for p in conv_p[:n_qp]], axis=-1) * DK ** -0.5
    kn = jnp.concatenate([l2n_heads(p) for p in conv_p[n_qp:2 * n_qp]], axis=-1)
    vv = jnp.concatenate(conv_p[2 * n_qp:], axis=-1)
    qn_b = qn.astype(_BF16)
    kn_b = kn.astype(_BF16)

    wide = HEADS * chunk
    lsh = int(math.log2(chunk))
    w_lane = lax.broadcasted_iota(jnp.int32, (chunk, wide), 1)
    w_row = lax.broadcasted_iota(jnp.int32, (chunk, wide), 0)
    w_head = w_lane >> lsh
    w_pos = w_lane & (chunk - 1)
    incl_w = w_pos <= w_row
    strict_w = w_pos < w_row
    eye_w = jnp.where(w_pos == w_row, 1.0, 0.0)
    bd_b = bd_ref[...]
    bdk_b = bdk_ref[...]
    code_w = codew_ref[...]

    def block_diag(xb):
        return jnp.concatenate([xb] * HEADS, axis=0) * bd_b

    def head_cols(arr, col0, rs):
        out = jnp.broadcast_to(arr[rs, col0:col0 + 1], (chunk, wide))
        for hd in range(1, HEADS):
            out = jnp.where(w_head == hd, arr[rs, col0 + hd:col0 + hd + 1], out)
        return out

    items = []
    for c in range(n_chunks):
        rs = slice(c * chunk, (c + 1) * chunk)
        k_c = kn_b[rs]
        bdk = jnp.concatenate([k_c] * HEADS, axis=0) * bdk_b
        kq = _dot_nt(jnp.concatenate([k_c, qn_b[rs]], axis=0), bdk)
        gc_w = head_cols(g_gd, 0, rs)
        bc_w = head_cols(beta, HEADS, rs)
        gr_w = jnp.concatenate([g_gd_t[hd:hd + 1, rs] for hd in range(HEADS)], axis=1)
        rel_w = jnp.exp(jnp.where(incl_w, gc_w - gr_w, -jnp.inf))
        p_w = jnp.where(strict_w, -(bc_w * rel_w * kq[:chunk]), 0.0)
        items.append(dict(c=c, rs=rs, p=p_w, t=eye_w + jnp.where(code_w == 0, p_w, 0.0),
                          qk=jnp.where(incl_w, kq[chunk:] * rel_w, 0.0).astype(_BF16)))
        next(ffn_fill, None)

    def wy_chain():
        for sh in range(1, lsh):
            for it in items:
                it["xb"] = it["t"].astype(_BF16)
                minus_l = jnp.where(code_w == sh, it["p"], 0.0)
                it["z"] = _dot(minus_l, block_diag(it["xb"])).astype(_BF16)
            yield
            for it in items:
                it["t"] = it["t"] + _dot(it["xb"], block_diag(it["z"]))
            yield

    heads = []

    def solve_chain():
        for it in items:
            rs = it["rs"]
            tb = it["t"].astype(_BF16)
            a_w = eye_w - it["p"]
            for hd in range(HEADS):
                ls = slice(hd * DK, (hd + 1) * DK)
                ws_ = slice(hd * chunk, (hd + 1) * chunk)
                gc = g_gd[rs, hd:hd + 1]
                bc = beta[rs, HEADS + hd:HEADS + hd + 1]
                eg = jnp.exp(gc)
                g_end = gc[chunk - 1:chunk, :]
                k = kn[rs, ls]
                heads.append(dict(
                    c=it["c"], hd=hd, x0=tb[:, ws_], a=a_w[:, ws_],
                    rhs=jnp.concatenate([bc * vv[rs, ls], (bc * eg) * k], axis=-1),
                    qk=it["qk"][:, ws_],
                    qe=(qn[rs, ls] * eg).astype(_BF16),
                    kdec=(k * jnp.exp(g_end - gc)).astype(_BF16),
                    dec=jnp.exp(g_end)))
        for it in heads:
            it["sol0"] = _dot(it["x0"], it["rhs"])
        yield
        for it in heads:
            it["resid"] = it["rhs"] - _dot_hl(it["a"], it["sol0"])
        yield
        for it in heads:
            sol = it["sol0"] + _dot(it["x0"], it["resid"])
            it["u0"] = sol[:, :DV]
            it["wq"] = jnp.concatenate([sol[:, DV:].astype(_BF16), it["qe"]], axis=0)
        yield

    def gdn_chain():
        for j in range(chunks_per_seq):
            cs = [sq * chunks_per_seq + j for sq in range(n_seq)]
            its = [it for c in cs for it in heads[c * HEADS:(c + 1) * HEADS]]
            wss = []
            for it in its:
                sq = it["c"] // chunks_per_seq
                wss.append(_dot(it["wq"], s_gd[sq][it["hd"]]))
            yield
            for it, ws in zip(its, wss):
                sq = it["c"] // chunks_per_seq
                hd = it["hd"]
                u = (it["u0"] - ws[:chunk]).astype(_BF16)
                o_gd[hd][it["c"]] = ws[chunk:] + _dot(it["qk"], u)
                s_gd[sq][hd] = it["dec"] * s_gd[sq][hd] + _dot_tn(it["kdec"], u)
            yield

    side = {}

    def side_steps():
        for name, w_ref, lo, hi_ in (("hog", w1_ref, 3 * QK, 4 * QK), ("gz", w2_ref, 0, QK),
                                     ("gate_a", w2_ref, QK, QK + D_MODEL),
                                     ("gate_b", w2_ref, QK + D_MODEL, QK + 2 * D_MODEL)):
            for c0 in range(lo, hi_, MXU_COLS):
                side.setdefault(name, []).append(_dot(hb, w_ref[:, c0:c0 + MXU_COLS]))
                yield

    def level_steps():
        for m in level_sizes:
            level_prep(m)
            yield

    n_head = n_hchunks * (len(level_sizes) + 2)
    n_side = (2 * QK + 2 * D_MODEL) // MXU_COLS
    _run_interleaved(wy_chain(), 2 * (lsh - 1),
                     _chained(_alternating(level_steps(), side_steps()), hgrn_head_steps(0)),
                     len(level_sizes) + n_side + n_head)
    _run_interleaved(_chained(solve_chain(), gdn_chain()), 3 + 2 * chunks_per_seq,
                     _alternating(_chained(*[hgrn_head_steps(hd) for hd in range(1, HEADS)]), ffn_fill),
                     (HEADS - 1) * n_head + (n_ffn if ffn is not None else 0))
    side = {k: jnp.concatenate(v, axis=-1) for k, v in side.items()}

    def head_norm(o, wn):
        parts = []
        for hd in range(HEADS):
            parts.append(_rms_rows(o[:, hd * DV:(hd + 1) * DV]) * wn)
        return jnp.concatenate(parts, axis=-1)

    for sq in range(n_seq):
        for hd in range(HEADS):
            if ffn is not None:
                shg_ref[sq, hd] = jnp.where(live, s_hg[sq][hd], shg_ref[sq, hd])
                sgd_ref[sq, hd] = jnp.where(live, s_gd[sq][hd], sgd_ref[sq, hd])
            else:
                shg_ref[sq, hd] = s_hg[sq][hd]
                sgd_ref[sq, hd] = s_gd[sq][hd]
    o_hg_all = jnp.concatenate(o_hg, axis=-1)
    o_gd_all = jnp.concatenate([jnp.concatenate(o_gd[hd], axis=0) for hd in range(HEADS)], axis=-1)
    o_a = head_norm(o_hg_all, hgn_ref[...]) * _silu(side["hog"])
    o_b = head_norm(o_gd_all, gdn_ref[...]) * _silu(side["gz"])
    y_a = _dot(o_a, wpa_ref[...])
    y_b = _dot(o_b, wpb_ref[...])
    ym = _sigmoid(side["gate_a"]) * y_a + _sigmoid(side["gate_b"]) * y_b
    out = _dot(ym, wout_ref[...])
    g1 = _bcast_rows(mod_ref, 2, bt, ts)
    x_new = (x + g1 * out).reshape(bt, ts, D_MODEL)
    if ffn is not None:
        for _ in ffn_fill:
            pass
        xmid_ref[...] = x_new
        xo_ref[...] = ffn_out[0].reshape(bt, ts, D_MODEL)
    else:
        xo_ref[...] = x_new


def _mixer_constants(r, chunk, hchunk):
    t = np.arange(r)[:, None]
    s = np.arange(r)[None, :]

    def block_tri(size):
        return ((t // size == s // size) & (s <= t)).astype(np.float32)

    top_bit = np.floor(np.log2(np.maximum(t ^ s, 1))).astype(np.int32)
    code = np.where(s < t, top_bit, np.where(s == t, -1, -2)).astype(np.int32)
    wide = HEADS * chunk
    wr = np.arange(wide)[:, None] // chunk
    bd = (wr == np.arange(wide)[None, :] // chunk).astype(np.float32)
    bdk = (wr == np.arange(QK)[None, :] // DK).astype(np.float32)
    code_wide = np.tile(code[:chunk, :chunk], (1, HEADS))
    return [jnp.asarray(block_tri(hchunk), _BF16), jnp.asarray(block_tri(chunk), _BF16),
            jnp.asarray(code[:hchunk, :hchunk]), jnp.asarray(bd, _BF16), jnp.asarray(bdk, _BF16),
            jnp.asarray(code_wide)]


def _const_spec(shape):
    nd = len(shape)
    return pl.BlockSpec(shape, lambda *_: (0,) * nd, pipeline_mode=pl.Buffered(1))


def _mixer(x, mod, mod_row0, states, hg_lb, lw, *, layer, bt, ts, chunk, hchunk):
    b, t, _ = x.shape
    carry = states is None
    nb = b // bt
    nt = t // ts
    if carry:
        grid = (nb, nt)
        bmap3 = lambda i, j: (i, j, 0)
        smap4 = lambda i, j: (i, 0, 0, 0)
        smap3 = lambda i, j: (i, 0, 0)
        modmap = lambda i, j: (layer, mod_row0 // bt + i, 0, 0)
        sem = ("arbitrary", "arbitrary")
    else:
        assert nt == 1 and ts == chunk
        grid = (nb,)
        bmap3 = lambda i: (i, 0, 0)
        smap4 = lambda i: (i, 0, 0, 0)
        smap3 = lambda i: (i, 0, 0)
        modmap = lambda i: (layer, mod_row0 // bt + i, 0, 0)
        sem = ("arbitrary",)

    in_specs = [pl.BlockSpec((bt, ts, D_MODEL), bmap3),
                pl.BlockSpec((None, bt, N_MOD, D_MODEL), modmap)]
    args = [x, mod]
    if not carry:
        in_specs += [pl.BlockSpec((None, bt, HEADS, DK, DV), lambda i: (layer, i, 0, 0, 0)),
                     pl.BlockSpec((None, bt, HEADS, DK, DV), lambda i: (layer, i, 0, 0, 0)),
                     pl.BlockSpec((None, bt, CONV_W - 1, CONV_CH), lambda i: (layer, i, 0, 0))]
        args += list(states)
    consts = [hg_lb, lw["norm_mix"], lw["w1"], lw["wab"], lw["w2"], lw["hg_norm"], lw["conv_w"], lw["a_log"],
              lw["dt_bias"], lw["gdn_norm"], lw["w_pa"], lw["w_pb"], lw["w_out"]]
    consts += _mixer_constants(bt * ts, chunk, hchunk)
    in_specs += [_const_spec(a.shape) for a in consts]
    args += consts

    out_shape = (jax.ShapeDtypeStruct((b, t, D_MODEL), _F32),
                 jax.ShapeDtypeStruct((b, HEADS, DK, DV), _F32),
                 jax.ShapeDtypeStruct((b, HEADS, DK, DV), _F32),
                 jax.ShapeDtypeStruct((b, CONV_W - 1, CONV_CH), _F32))
    out_specs = (pl.BlockSpec((bt, ts, D_MODEL), bmap3),
                 pl.BlockSpec((bt, HEADS, DK, DV), smap4),
                 pl.BlockSpec((bt, HEADS, DK, DV), smap4),
                 pl.BlockSpec((bt, CONV_W - 1, CONV_CH), smap3))
    return pl.pallas_call(
        functools.partial(_mixer_kernel, layer=layer, bt=bt, ts=ts, chunk=chunk, hchunk=hchunk, carry=carry),
        grid=grid, in_specs=in_specs, out_specs=out_specs, out_shape=out_shape,
        scratch_shapes=[pltpu.VMEM((bt, PAD + ts, CONV_CH), _F32)],
        compiler_params=pltpu.CompilerParams(dimension_semantics=sem, vmem_limit_bytes=VMEM_LIMIT),
        name=f"mixer_l{layer}_{'prompt' if carry else 'sample'}",
    )(*args)


def _fused_layer(x, mod, lw, hg_lb, final_norm, *, layer, ts, chunk, hchunk, last):
    b, t, _ = x.shape
    nt = t // ts
    n_tiles = b * nt
    mix_tile = lambda g: jnp.minimum(g, n_tiles - 1)
    ffn_tile = lambda g: jnp.maximum(g - 1, 0)
    consts = [hg_lb, lw["norm_mix"], lw["w1"], lw["wab"], lw["w2"], lw["hg_norm"], lw["conv_w"], lw["a_log"],
              lw["dt_bias"], lw["gdn_norm"], lw["w_pa"], lw["w_pb"], lw["w_out"]]
    consts += _mixer_constants(ts, chunk, hchunk)
    consts += [lw["norm_ffn"], lw["w_up"], lw["w_down"], final_norm]
    in_specs = [pl.BlockSpec((1, ts, D_MODEL), lambda g: (mix_tile(g) // nt, mix_tile(g) % nt, 0)),
                pl.BlockSpec((None, 1, N_MOD, D_MODEL), lambda g: (layer, mix_tile(g) // nt, 0, 0)),
                pl.BlockSpec((None, 1, N_MOD, D_MODEL), lambda g: (layer, ffn_tile(g) // nt, 0, 0))]
    in_specs += [_const_spec(a.shape) for a in consts]
    smap4 = lambda g: (mix_tile(g) // nt, 0, 0, 0)
    smap3 = lambda g: (mix_tile(g) // nt, 0, 0)
    out_shape = (jax.ShapeDtypeStruct((b, t, D_MODEL), _F32),
                 jax.ShapeDtypeStruct((b, HEADS, DK, DV), _F32),
                 jax.ShapeDtypeStruct((b, HEADS, DK, DV), _F32),
                 jax.ShapeDtypeStruct((b, CONV_W - 1, CONV_CH), _F32))
    out_specs = (pl.BlockSpec((1, ts, D_MODEL), lambda g: (ffn_tile(g) // nt, ffn_tile(g) % nt, 0)),
                 pl.BlockSpec((1, HEADS, DK, DV), smap4),
                 pl.BlockSpec((1, HEADS, DK, DV), smap4),
                 pl.BlockSpec((1, CONV_W - 1, CONV_CH), smap3))
    return pl.pallas_call(
        functools.partial(_mixer_kernel, layer=layer, bt=1, ts=ts, chunk=chunk, hchunk=hchunk, carry=True,
                          ffn=(nt, n_tiles, last)),
        grid=(n_tiles + 1,), in_specs=in_specs, out_specs=out_specs, out_shape=out_shape,
        scratch_shapes=[pltpu.VMEM((1, PAD + ts, CONV_CH), _F32),
                        pltpu.VMEM((1, ts, D_MODEL), _F32),
                        pltpu.VMEM((ts, D_FF), _BF16)],
        compiler_params=pltpu.CompilerParams(dimension_semantics=("arbitrary",),
                                             vmem_limit_bytes=FUSED_VMEM_LIMIT),
        name=f"layer_l{layer}_prompt",
    )(x, mod, mod, *consts)


FF_HALF = D_FF // 2


def _ffn_kernel(x_ref, mod_ref, nffn_ref, wup_ref, wdn_ref, fin_ref, o_ref, act_ref, *, bt, ts, last):
    r = bt * ts
    x = x_ref[...].reshape(r, D_MODEL)
    sh2 = _bcast_rows(mod_ref, 3, bt, ts)
    sc2 = _bcast_rows(mod_ref, 4, bt, ts)
    g2 = _bcast_rows(mod_ref, 5, bt, ts)
    hb = (_rms_rows(x) * nffn_ref[...] * (1.0 + sc2) + sh2).astype(_BF16)
    for j in range(2):
        lo = j * FF_HALF
        gate = jnp.dot(hb, wup_ref[:, lo:lo + FF_HALF], preferred_element_type=_F32)
        up = jnp.dot(hb, wup_ref[:, D_FF + lo:D_FF + lo + FF_HALF], preferred_element_type=_F32)
        act_ref[:, lo:lo + FF_HALF] = (_silu(gate) * up).astype(_BF16)
    out = jnp.dot(act_ref[...], wdn_ref[...], preferred_element_type=_F32)
    xn = x + g2 * out
    if last:
        xn = _rms_rows(xn) * fin_ref[...]
    o_ref[...] = xn.reshape(bt, ts, D_MODEL)


def _ffn(x, mod, mod_row0, lw, final_norm, *, layer, bt, ts, last):
    b, t, _ = x.shape
    grid = (b // bt, t // ts)
    consts = [lw["norm_ffn"], lw["w_up"], lw["w_down"], final_norm]
    r = bt * ts
    return pl.pallas_call(
        functools.partial(_ffn_kernel, bt=bt, ts=ts, last=last),
        grid=grid,
        in_specs=[pl.BlockSpec((bt, ts, D_MODEL), lambda i, j: (i, j, 0)),
                  pl.BlockSpec((None, bt, N_MOD, D_MODEL), lambda i, j: (layer, mod_row0 // bt + i, 0, 0))]
                 + [_const_spec(a.shape) for a in consts],
        out_specs=pl.BlockSpec((bt, ts, D_MODEL), lambda i, j: (i, j, 0)),
        out_shape=jax.ShapeDtypeStruct((b, t, D_MODEL), _F32),
        scratch_shapes=[pltpu.VMEM((r, D_FF), _BF16)],
        compiler_params=pltpu.CompilerParams(dimension_semantics=("arbitrary", "arbitrary"),
                                             vmem_limit_bytes=VMEM_LIMIT),
        name=f"ffn_l{layer}",
    )(x, mod, *consts)


def _cast_kernel(w_ref, o_ref):
    o_ref[...] = w_ref[...].astype(_BF16)


def _to_bf16(w, layer, cols=None):
    _, rows, n = w.shape
    cols = n if cols is None else cols
    tr = rows // 2
    return pl.pallas_call(
        _cast_kernel,
        grid=(rows // tr,),
        in_specs=[pl.BlockSpec((None, tr, cols), lambda i: (layer, i, 0))],
        out_specs=pl.BlockSpec((tr, cols), lambda i: (i, 0)),
        out_shape=jax.ShapeDtypeStruct((rows, cols), _BF16),
        compiler_params=pltpu.CompilerParams(dimension_semantics=("arbitrary",), vmem_limit_bytes=VMEM_LIMIT),
        name="cast_bf16",
    )(w)


def _layer_weights(l, norm_mix, w_in, hg_norm, conv_w, gdn_a_log, gdn_dt_bias, gdn_norm, w_proj_a, w_proj_b,
                   w_out, norm_ffn, w_up, w_down):
    pad_lanes = lambda v, off: jnp.zeros((1, 128), _F32).at[0, off:off + HEADS].set(v.astype(_F32))
    wl = w_in[l]
    wab = jnp.zeros((D_MODEL, 128), _BF16).at[:, :2 * HEADS].set(wl[:, _C_AB:_C_REST].astype(_BF16))
    return {
        "norm_mix": norm_mix[l].reshape(1, D_MODEL),
        "w1": _to_bf16(w_in, l, _C_AB),
        "wab": wab,
        "w2": wl[:, _C_REST:].astype(_BF16),
        "hg_norm": hg_norm[l].reshape(1, DV),
        "conv_w": conv_w[l],
        "a_log": pad_lanes(gdn_a_log[l], 0),
        "dt_bias": pad_lanes(gdn_dt_bias[l], 0),
        "gdn_norm": gdn_norm[l].reshape(1, DV),
        "w_pa": w_proj_a[l].astype(_BF16),
        "w_pb": w_proj_b[l].astype(_BF16),
        "w_out": w_out[l].astype(_BF16),
        "norm_ffn": norm_ffn[l].reshape(1, D_MODEL),
        "w_up": _to_bf16(w_up, l),
        "w_down": _to_bf16(w_down, l),
    }


def kernel(x_prompt, x_sample, state_hgrn, state_gdn, state_conv, c_prompt, c_sample, w_ada, b_ada, norm_mix,
           w_in, hg_lb, hg_norm, conv_w, gdn_a_log, gdn_dt_bias, gdn_norm, w_proj_a, w_proj_b, w_out, norm_ffn,
           w_up, w_down, final_norm):
    n_prompt = x_prompt.shape[0]
    n_sample, t_sample, _ = x_sample.shape
    c_all = jnp.concatenate([c_prompt, c_sample], axis=0)
    mod = _modulation(c_all, w_ada, b_ada).reshape(DEPTH, n_prompt + n_sample, N_MOD, D_MODEL)
    fin = final_norm.reshape(1, D_MODEL)
    hg_lb = hg_lb.astype(_F32)

    xp, xs = x_prompt, x_sample
    hg_p, gd_p, cv_p, hg_s, gd_s, cv_s = [], [], [], [], [], []
    for l in range(DEPTH):
        lw = _layer_weights(l, norm_mix, w_in, hg_norm, conv_w, gdn_a_log, gdn_dt_bias, gdn_norm, w_proj_a,
                            w_proj_b, w_out, norm_ffn, w_up, w_down)
        last = l == DEPTH - 1
        xp, a, b, c = _fused_layer(xp, mod, lw, hg_lb, fin, layer=l, ts=256, chunk=64, hchunk=128, last=last)
        hg_p.append(a); gd_p.append(b); cv_p.append(c)
        xs, a, b, c = _mixer(xs, mod, n_prompt, (state_hgrn, state_gdn, state_conv), hg_lb, lw,
                             layer=l, bt=8, ts=t_sample, chunk=t_sample, hchunk=t_sample)
        hg_s.append(a); gd_s.append(b); cv_s.append(c)
        xs = _ffn(xs, mod, n_prompt, lw, fin, layer=l, bt=8, ts=t_sample, last=last)
    st = jnp.stack
    return (xp, xs, st(hg_p), st(gd_p), st(cv_p), st(hg_s), st(gd_s), st(cv_s))
```
